```python
import math
import jax
import jax.numpy as jnp
from jax import lax
import numpy as np

D_MODEL = 1024
BATCH = 1
SEQ = 16384
DEPTH = 2
DEC_BATCH = 32
DEC_SEQ = 4
PAST_LEN = 16384
PAGE_SIZE = 128

MLSTM_WIDTH = 3 * D_MODEL // 8
M_HEADS = 4
M_DK = MLSTM_WIDTH // M_HEADS
M_DV = MLSTM_WIDTH // M_HEADS
M_CHUNK = 64
ATTN_WIDTH = 3 * D_MODEL // 8
A_HEADS = 6
HEAD_DIM = ATTN_WIDTH // A_HEADS
IDX_HEADS = 4
IDX_DIM = 64
TOPK_MAX = 256
Q_BLOCK = 128
ROPE_THETA = 500000.0
ROPE_FRACTION = 4
POOL_WIDTH = D_MODEL - MLSTM_WIDTH - ATTN_WIDTH
POOL_WINDOWS = (2, 4, 8, 16)
POOL_GROUPS = len(POOL_WINDOWS)
POOL_GDIM = POOL_WIDTH // POOL_GROUPS
POOL_BUF = max(POOL_WINDOWS) - 1
D_FF = 4 * D_MODEL
EPS = 1e-6
NEG = -1e30
SPLIT_SIZES = (MLSTM_WIDTH, MLSTM_WIDTH, MLSTM_WIDTH, MLSTM_WIDTH, M_HEADS, M_HEADS,
               ATTN_WIDTH, ATTN_WIDTH, ATTN_WIDTH, IDX_HEADS * IDX_DIM, IDX_DIM, IDX_HEADS,
               POOL_WIDTH)
IN_WIDTH = sum(SPLIT_SIZES)
SPLIT_POINTS = tuple(int(s) for s in np.cumsum(SPLIT_SIZES)[:-1])

kernel_name = 'hymba_mlstm_dsa_pool_step'

F32 = jnp.float32


def rmsnorm(x, g):
    xf = x.astype(F32)
    y = xf * lax.rsqrt(jnp.mean(xf * xf, axis=-1, keepdims=True) + EPS)
    return (y * g.astype(F32)).astype(x.dtype)


def rope(x, pos):
    d = x.shape[-1]
    rot = d // ROPE_FRACTION
    half = rot // 2
    inv = ROPE_THETA ** (-jnp.arange(half, dtype=F32) * 2.0 / rot)
    ang = pos.astype(F32)[:, None] * inv[None, :]
    cos = jnp.cos(ang)[:, None, :]
    sin = jnp.sin(ang)[:, None, :]
    xf = x.astype(F32)
    x1, x2 = xf[..., :half], xf[..., half:rot]
    out = jnp.concatenate([x1 * cos - x2 * sin, x1 * sin + x2 * cos, xf[..., rot:]], axis=-1)
    return out.astype(x.dtype)


def take_rows(a, idx):
    return jax.vmap(lambda ab, ib: ab[ib])(a, idx)


def prep(x, pos, g_mix, w_in, b_i, b_f, g_q, g_k):
    b, t, _ = x.shape
    z = rmsnorm(x, g_mix) @ w_in
    mq, mk, mv, mo, mi, mf, aq, ak, av, iq, ik, iw, pu = jnp.split(z, SPLIT_POINTS, axis=-1)
    heads_bhtd = lambda a: a.reshape(b, t, M_HEADS, -1).transpose(0, 2, 1, 3).astype(F32)
    mq = heads_bhtd(mq)
    mk = heads_bhtd(mk) * (M_DK ** -0.5)
    mv = heads_bhtd(mv)
    ig = (mi + b_i).astype(F32).transpose(0, 2, 1)
    lf = jax.nn.log_sigmoid((mf + b_f).astype(F32)).transpose(0, 2, 1)
    aq = rope(rmsnorm(aq.reshape(b, t, A_HEADS, HEAD_DIM), g_q), pos)
    ak = rope(rmsnorm(ak.reshape(b, t, A_HEADS, HEAD_DIM), g_k), pos)
    av = av.reshape(b, t, A_HEADS, HEAD_DIM)
    iq = rope(iq.reshape(b, t, IDX_HEADS, IDX_DIM), pos)
    ik = rope(ik[:, :, None, :], pos)[:, :, 0, :]
    iw = iw.astype(F32) * (IDX_HEADS ** -0.5) * (IDX_DIM ** -0.5)
    return (mq, mk, mv, mo, ig, lf), (aq, ak, av, iq, ik, iw), pu


def mlstm_chunk(state, q, k, v, ig, lf):
    c, n, m = state
    l = q.shape[2]
    bcum = jnp.cumsum(lf, axis=-1)
    causal = jnp.tril(jnp.ones((l, l), dtype=bool))
    log_d = bcum[..., :, None] - bcum[..., None, :] + ig[..., None, :]
    log_d = jnp.where(causal, log_d, -jnp.inf)
    inter = bcum + m[..., None]
    m_t = jnp.maximum(inter, jnp.max(log_d, axis=-1))
    s = jnp.einsum('bhtd,bhsd->bhts', q, k) * jnp.exp(log_d - m_t[..., None])
    inter_w = jnp.exp(inter - m_t)
    num = jnp.einsum('bhts,bhsv->bhtv', s, v) + inter_w[..., None] * jnp.einsum('bhtd,bhdv->bhtv', q, c)
    den = jnp.sum(s, axis=-1) + inter_w * jnp.einsum('bhtd,bhd->bht', q, n)
    h = num / jnp.maximum(jnp.abs(den), jnp.exp(-m_t))[..., None]
    b_last = bcum[..., -1]
    log_w = b_last[..., None] - bcum + ig
    m_new = jnp.maximum(b_last + m, jnp.max(log_w, axis=-1))
    w_exp = jnp.exp(log_w - m_new[..., None])
    decay = jnp.exp(b_last + m - m_new)
    c_new = decay[..., None, None] * c + jnp.einsum('bhs,bhsd,bhsv->bhdv', w_exp, k, v)
    n_new = decay[..., None] * n + jnp.einsum('bhs,bhsd->bhd', w_exp, k)
    return (c_new, n_new, m_new), h


def mlstm_prompt(q, k, v, ig, lf):
    b, h, s, _ = q.shape
    nc = s // M_CHUNK
    chunks = lambda a: jnp.moveaxis(a.reshape(a.shape[:2] + (nc, M_CHUNK) + a.shape[3:]), 2, 0)
    state0 = (jnp.zeros((b, h, M_DK, M_DV), F32), jnp.zeros((b, h, M_DK), F32), jnp.zeros((b, h), F32))
    state, hs = lax.scan(lambda st, xs: mlstm_chunk(st, *xs), state0,
                         (chunks(q), chunks(k), chunks(v), chunks(ig), chunks(lf)))
    hs = jnp.moveaxis(hs, 0, 2).reshape(b, h, s, M_DV)
    return hs, state


def mlstm_out(h, mo, g_mhead):
    b, _, t, _ = h.shape
    o = jax.nn.sigmoid(mo.astype(F32)).reshape(b, t, M_HEADS, M_DV)
    y = rmsnorm(o * h.transpose(0, 2, 1, 3), g_mhead.reshape(M_HEADS, M_DV))
    return y.reshape(b, t, MLSTM_WIDTH).astype(mo.dtype)


def dsa_attend(q, qi, w, qpos, ik_all, gather):
    n_keys = ik_all.shape[1]
    topk = min(TOPK_MAX, n_keys // 4)
    sc = jnp.einsum('bthd,bsd->bths', qi.astype(F32), ik_all.astype(F32))
    score = jnp.einsum('bths,bth->bts', jax.nn.relu(sc), w)
    admissible = jnp.arange(n_keys)[None, :] <= qpos[:, None]
    score = jnp.where(admissible[None], score, NEG)
    _, idx = lax.top_k(score, topk)
    valid = idx <= qpos[None, :, None]
    k_sel, v_sel = gather(idx)
    logits = jnp.einsum('bthd,btkhd->bthk', q, k_sel).astype(F32) * (HEAD_DIM ** -0.5)
    logits = jnp.where(valid[:, :, None, :], logits, -jnp.inf)
    p = jax.nn.softmax(logits, axis=-1)
    return jnp.einsum('bthk,btkhd->bthd', p.astype(v_sel.dtype), v_sel)


def dsa_prompt(aq, ak, av, iq, ik, iw):
    b, s = aq.shape[:2]
    nb = s // Q_BLOCK
    to_blocks = lambda a: jnp.swapaxes(a.reshape((b, nb, Q_BLOCK) + a.shape[2:]), 0, 1)
    gather = lambda idx: (take_rows(ak, idx), take_rows(av, idx))
    def block(args):
        q, qi, w, qpos = args
        return dsa_attend(q, qi, w, qpos, ik, gather)
    pos_blocks = jnp.arange(s, dtype=jnp.int32).reshape(nb, Q_BLOCK)
    out = lax.map(block, (to_blocks(aq), to_blocks(iq), to_blocks(iw), pos_blocks))
    return jnp.swapaxes(out, 0, 1).reshape(b, s, ATTN_WIDTH)


def dsa_sample(aq, ak, av, iq, ik, iw, cache_k, cache_v, cache_idx_k, layer, page_table, pos):
    db, t = aq.shape[:2]
    page = cache_k.shape[2]
    past_len = page_table.shape[1] * page
    past_ik = cache_idx_k[layer, page_table].reshape(db, past_len, IDX_DIM)
    ik_all = jnp.concatenate([past_ik.astype(ik.dtype), ik], axis=1)
    def gather(idx):
        is_past = (idx < past_len)[..., None, None]
        pidx = jnp.minimum(idx, past_len - 1)
        phys = take_rows(page_table, pidx // page)
        off = pidx % page
        nidx = jnp.clip(idx - past_len, 0, t - 1)
        k_sel = jnp.where(is_past, cache_k[layer, phys, off].astype(ak.dtype), take_rows(ak, nidx))
        v_sel = jnp.where(is_past, cache_v[layer, phys, off].astype(av.dtype), take_rows(av, nidx))
        return k_sel, v_sel
    return dsa_attend(aq, iq, iw, pos, ik_all, gather).reshape(db, t, ATTN_WIDTH)


def pool_mix(hist, u, pos, w_pool, pool_scale):
    b, t, _ = u.shape
    p = hist.shape[1]
    ext = jnp.concatenate([hist.astype(F32), u.astype(F32)], axis=1)
    cs = jnp.concatenate([jnp.zeros((b, 1, POOL_WIDTH), F32), jnp.cumsum(ext, axis=1)], axis=1)
    end = cs[:, p + 1:]
    cur = ext[:, p:]
    diffs = []
    for g, w in enumerate(POOL_WINDOWS):
        lo, hi = g * POOL_GDIM, (g + 1) * POOL_GDIM
        start = cs[:, p + 1 - w:p + 1 - w + t, lo:hi]
        cnt = jnp.minimum(w, pos + 1).astype(F32)[None, :, None]
        diffs.append((end[..., lo:hi] - start) / cnt - cur[..., lo:hi])
    d = jnp.stack(diffs, axis=2)
    y = jnp.einsum('btgc,gcd->btgd', d, w_pool.astype(F32)).reshape(b, t, POOL_WIDTH)
    y = y * pool_scale.astype(F32)
    return y.astype(u.dtype), ext[:, -POOL_BUF:].astype(u.dtype)


def finish(x, mix_m, mix_a, mix_p, w_out, g_mlp, w_up, w_down):
    mix = jnp.concatenate([mix_m.astype(x.dtype), mix_a.astype(x.dtype), mix_p.astype(x.dtype)], axis=-1)
    x = x + mix @ w_out
    hid = rmsnorm(x, g_mlp) @ w_up
    return x + jnp.square(jax.nn.relu(hid)) @ w_down


def setup_inputs(seed: int = 0) -> dict:
    key = jax.random.key(seed)
    ks = jax.random.split(key, 24)
    n_pages = PAST_LEN // PAGE_SIZE
    used = DEC_BATCH * n_pages
    n_pool = used + used // 4
    nrm = lambda k, shape, s=1.0: s * jax.random.normal(k, shape, F32)
    page_table = jax.random.permutation(ks[5], n_pool)[:used].reshape(DEC_BATCH, n_pages).astype(jnp.int32)
    return {
        'x_prompt': nrm(ks[0], (BATCH, SEQ, D_MODEL)),
        'x_sample': nrm(ks[1], (DEC_BATCH, DEC_SEQ, D_MODEL)),
        'cache_k': nrm(ks[2], (DEPTH, n_pool, PAGE_SIZE, A_HEADS, HEAD_DIM)),
        'cache_v': nrm(ks[3], (DEPTH, n_pool, PAGE_SIZE, A_HEADS, HEAD_DIM)),
        'cache_idx_k': nrm(ks[4], (DEPTH, n_pool, PAGE_SIZE, IDX_DIM)),
        'state_C': nrm(ks[6], (DEPTH, DEC_BATCH, M_HEADS, M_DK, M_DV), 0.5),
        'state_n': nrm(ks[7], (DEPTH, DEC_BATCH, M_HEADS, M_DK), 0.5),
        'state_m': nrm(ks[8], (DEPTH, DEC_BATCH, M_HEADS)),
        'state_pool': nrm(ks[9], (DEPTH, DEC_BATCH, POOL_BUF, POOL_WIDTH)),
        'page_table': page_table,
        'g_mix': 1.0 + nrm(ks[10], (DEPTH, D_MODEL), 0.1),
        'w_in': nrm(ks[11], (DEPTH, D_MODEL, IN_WIDTH), D_MODEL ** -0.5),
        'b_i': nrm(ks[12], (DEPTH, M_HEADS), 0.1),
        'b_f': jnp.linspace(3.0, 6.0, M_HEADS)[None, :] + nrm(ks[13], (DEPTH, M_HEADS), 0.1),
        'g_q': 1.0 + nrm(ks[14], (DEPTH, HEAD_DIM), 0.1),
        'g_k': 1.0 + nrm(ks[15], (DEPTH, HEAD_DIM), 0.1),
        'g_mhead': 1.0 + nrm(ks[16], (DEPTH, MLSTM_WIDTH), 0.1),
        'w_pool': nrm(ks[17], (DEPTH, POOL_GROUPS, POOL_GDIM, POOL_GDIM), POOL_GDIM ** -0.5),
        'pool_scale': 1.0 + nrm(ks[18], (DEPTH, POOL_WIDTH), 0.1),
        'w_out': nrm(ks[19], (DEPTH, D_MODEL, D_MODEL), D_MODEL ** -0.5),
        'g_mlp': 1.0 + nrm(ks[20], (DEPTH, D_MODEL), 0.1),
        'w_up': nrm(ks[21], (DEPTH, D_MODEL, D_FF), D_MODEL ** -0.5),
        'w_down': nrm(ks[22], (DEPTH, D_FF, D_MODEL), D_FF ** -0.5),
    }


def reference(x_prompt, x_sample, cache_k, cache_v, cache_idx_k, state_C, state_n, state_m, state_pool,
              page_table, g_mix, w_in, b_i, b_f, g_q, g_k, g_mhead, w_pool, pool_scale, w_out, g_mlp,
              w_up, w_down):
    b, s, _ = x_prompt.shape
    db, t, _ = x_sample.shape
    past_len = page_table.shape[1] * cache_k.shape[2]
    pos_p = jnp.arange(s, dtype=jnp.int32)
    pos_s = past_len + jnp.arange(t, dtype=jnp.int32)
    xp, xs = x_prompt, x_sample
    kp, vp, ikp, cp, np_, mp, pp = [], [], [], [], [], [], []
    ksm, vsm, iks, cs_, ns, ms, ps = [], [], [], [], [], [], []
    for l in range(DEPTH):
        (mq, mk, mv, mo, ig, lf), (aq, ak, av, iq, ik, iw), pu = prep(
            xp, pos_p, g_mix[l], w_in[l], b_i[l], b_f[l], g_q[l], g_k[l])
        h, (c_new, n_new, m_new) = mlstm_prompt(mq, mk, mv, ig, lf)
        mix_m = mlstm_out(h, mo, g_mhead[l])
        mix_a = dsa_prompt(aq, ak, av, iq, ik, iw)
        hist0 = jnp.zeros((b, POOL_BUF, POOL_WIDTH), pu.dtype)
        mix_p, buf = pool_mix(hist0, pu, pos_p, w_pool[l], pool_scale[l])
        xp = finish(xp, mix_m, mix_a, mix_p, w_out[l], g_mlp[l], w_up[l], w_down[l])
        kp.append(ak); vp.append(av); ikp.append(ik)
        cp.append(c_new); np_.append(n_new); mp.append(m_new); pp.append(buf)
        (mq, mk, mv, mo, ig, lf), (aq, ak, av, iq, ik, iw), pu = prep(
            xs, pos_s, g_mix[l], w_in[l], b_i[l], b_f[l], g_q[l], g_k[l])
        st = (state_C[l].astype(F32), state_n[l].astype(F32), state_m[l].astype(F32))
        (c_new, n_new, m_new), h = mlstm_chunk(st, mq, mk, mv, ig, lf)
        mix_m = mlstm_out(h, mo, g_mhead[l])
        mix_a = dsa_sample(aq, ak, av, iq, ik, iw, cache_k, cache_v, cache_idx_k, l, page_table, pos_s)
        mix_p, buf = pool_mix(state_pool[l], pu, pos_s, w_pool[l], pool_scale[l])
        xs = finish(xs, mix_m, mix_a, mix_p, w_out[l], g_mlp[l], w_up[l], w_down[l])
        ksm.append(ak); vsm.append(av); iks.append(ik)
        cs_.append(c_new); ns.append(n_new); ms.append(m_new); ps.append(buf)
    stk = lambda a: jnp.stack(a, axis=0)
    return (xp, xs, stk(kp), stk(vp), stk(ikp), stk(cp), stk(np_), stk(mp), stk(pp),
            stk(ksm), stk(vsm), stk(iks), stk(cs_), stk(ns), stk(ms), stk(ps))
```

```python
import functools

import numpy as np
import jax
import jax.numpy as jnp
from jax import lax
from jax.experimental import pallas as pl
from jax.experimental.pallas import tpu as pltpu

F32 = jnp.float32
BF16 = jnp.bfloat16

D_MODEL = 1024
M_HEADS = 4
M_DK = 96
A_HEADS = 6
HEAD_DIM = 64
ATTN_WIDTH = A_HEADS * HEAD_DIM
IDX_HEADS = 4
IDX_DIM = 64
TOPK_MAX = 256
ROPE_THETA = 500000.0
ROPE_DIMS = HEAD_DIM // 4
POOL_WINDOWS = (2, 4, 8, 16)
POOL_GDIM = 64
POOL_WIDTH = 256
POOL_BUF = 15
D_FF = 4 * D_MODEL
EPS = 1e-6
NEG = -1e30
MLSTM_WIDTH = M_HEADS * M_DK
SPLIT_SIZES = (MLSTM_WIDTH, MLSTM_WIDTH, MLSTM_WIDTH, MLSTM_WIDTH, M_HEADS, M_HEADS,
               ATTN_WIDTH, ATTN_WIDTH, ATTN_WIDTH, IDX_HEADS * IDX_DIM, IDX_DIM, IDX_HEADS, POOL_WIDTH)
SPLIT_POINTS = tuple(int(s) for s in np.cumsum(SPLIT_SIZES)[:-1])

LANE = 128
MP = M_HEADS * LANE
N_COL = M_DK
OFF_MQ, OFF_MK, OFF_MV, OFF_MO = 0, MP, 2 * MP, 3 * MP
OFF_AQ = 4 * MP
OFF_AK = OFF_AQ + ATTN_WIDTH
OFF_AV = OFF_AK + ATTN_WIDTH
OFF_IQ = OFF_AV + ATTN_WIDTH
OFF_IK = OFF_IQ + IDX_HEADS * LANE
OFF_PU = OFF_IK + LANE
OFF_G = OFF_PU + POOL_WIDTH
W_PACKED = OFF_G + LANE
G_IG, G_LF, G_IW = 0, M_HEADS, 2 * M_HEADS

VMEM_LIMIT = 56 * 1024 * 1024
N_BISECT = 28
MASK_VALUE = -1e30

_NT = (((1,), (1,)), ((), ()))


def _cparams(sem):
    return pltpu.CompilerParams(dimension_semantics=sem, vmem_limit_bytes=VMEM_LIMIT)


def _const_spec(shape):
    nd = len(shape)
    return pl.BlockSpec(shape, lambda *_: (0,) * nd, pipeline_mode=pl.Buffered(1))


def _split3(a):
    a1 = a.astype(BF16)
    r1 = a - a1.astype(F32)
    a2 = r1.astype(BF16)
    a3 = (r1 - a2.astype(F32)).astype(BF16)
    return a1, a2, a3


def _dot_exact_lhs(a, b01):
    return sum(jnp.dot(t, b01, preferred_element_type=F32) for t in _split3(a))


def _dot_exact_rhs(a01, b):
    return sum(jnp.dot(a01, t, preferred_element_type=F32) for t in _split3(b))


def _rope_group(x, c, s, period):
    lane = lax.broadcasted_iota(jnp.int32, x.shape, 1)
    first_half = (lane % period) < (ROPE_DIMS // 2)
    partner = jnp.where(first_half, pltpu.roll(x, LANE - ROPE_DIMS // 2, 1), pltpu.roll(x, ROPE_DIMS // 2, 1))
    return x * c + partner * s


def _rope(x, c, s, period):
    groups = [_rope_group(x[:, g * LANE:(g + 1) * LANE], c, s, period) for g in range(x.shape[1] // LANE)]
    return groups[0] if len(groups) == 1 else jnp.concatenate(groups, axis=1)


def _prep_kernel(x_ref, g_ref, w_ref, bias_ref, gq_ref, gk_ref, c64_ref, s64_ref, c128_ref, s128_ref, bd_ref,
                 mq_o, mk_o, mv_o, mo_o, aq_o, ak32_o, akb_o, av32_o, avb_o, iq_o, ik32_o, ikb_o, pu_o,
                 gates_o, gates_t_o):
    x = x_ref[...]
    h = (x * lax.rsqrt(jnp.mean(x * x, axis=-1, keepdims=True) + EPS) * g_ref[...]).astype(BF16)

    def seg(off, width):
        return jnp.dot(h, w_ref[:, off:off + width], preferred_element_type=F32)

    tm = x.shape[0]
    lane_mp = lax.broadcasted_iota(jnp.int32, (tm, MP), 1)
    mq_o[...] = seg(OFF_MQ, MP).astype(BF16)
    mk_o[...] = (seg(OFF_MK, MP) * (M_DK ** -0.5)).astype(BF16)
    mv_o[...] = jnp.where(lane_mp % LANE == N_COL, 1.0, seg(OFF_MV, MP)).astype(BF16)
    mo_o[...] = seg(OFF_MO, MP)

    c64, s64, c128, s128 = c64_ref[...], s64_ref[...], c128_ref[...], s128_ref[...]
    bd = bd_ref[...]

    def qk_norm(z, gain):
        ms = _dot_exact_lhs(z * z, bd) * (1.0 / HEAD_DIM)
        return _rope(z * lax.rsqrt(ms + EPS) * gain, c64, s64, HEAD_DIM)

    aq = qk_norm(seg(OFF_AQ, ATTN_WIDTH), gq_ref[...])
    aq_o[...] = (aq * (HEAD_DIM ** -0.5)).astype(BF16)
    ak = qk_norm(seg(OFF_AK, ATTN_WIDTH), gk_ref[...])
    ak32_o[...] = ak
    akb_o[...] = ak.astype(BF16)
    av = seg(OFF_AV, ATTN_WIDTH)
    av32_o[...] = av
    avb_o[...] = av.astype(BF16)
    iq_o[...] = _rope(seg(OFF_IQ, IDX_HEADS * LANE), c128, s128, LANE).astype(BF16)
    ik = _rope(seg(OFF_IK, LANE), c128, s128, LANE)
    ik32_o[...] = ik[:, :IDX_DIM]
    ikb_o[...] = ik.astype(BF16)
    pu_o[...] = seg(OFF_PU, POOL_WIDTH)

    zg = seg(OFF_G, LANE)
    zb = zg + bias_ref[...]
    lane = lax.broadcasted_iota(jnp.int32, zg.shape, 1)
    log_sig = jnp.minimum(zb, 0.0) - jnp.log1p(jnp.exp(-jnp.abs(zb)))
    iw_scale = (IDX_HEADS ** -0.5) * (IDX_DIM ** -0.5)
    gates = jnp.where(lane < G_LF, zb,
                      jnp.where(lane < G_IW, log_sig,
                                jnp.where(lane < G_IW + IDX_HEADS, zg * iw_scale, 0.0)))
    gates_o[...] = gates
    gates_t_o[...] = gates.T[:8, :]


def _prep_call(x, pos_tabs, lw):
    t = x.shape[0]
    tm = min(256, t)
    row = lambda w: pl.BlockSpec((tm, w), lambda i: (i, 0))
    out_shapes = dict(
        mq=(MP, BF16), mk=(MP, BF16), mv=(MP, BF16), mo=(MP, F32),
        aq=(ATTN_WIDTH, BF16), ak32=(ATTN_WIDTH, F32), akb=(ATTN_WIDTH, BF16),
        av32=(ATTN_WIDTH, F32), avb=(ATTN_WIDTH, BF16),
        iq=(IDX_HEADS * LANE, BF16), ik32=(IDX_DIM, F32), ikb=(LANE, BF16),
        pu=(POOL_WIDTH, F32), gates=(LANE, F32))
    names = list(out_shapes)
    out_shape = [jax.ShapeDtypeStruct((t, w), dt) for w, dt in out_shapes.values()]
    out_specs = [row(w) for w, _ in out_shapes.values()]
    out_shape.append(jax.ShapeDtypeStruct((8, t), F32))
    out_specs.append(pl.BlockSpec((8, tm), lambda i: (0, i)))
    outs = pl.pallas_call(
        _prep_kernel,
        grid=(t // tm,),
        in_specs=[row(D_MODEL), _const_spec((1, D_MODEL)), _const_spec((D_MODEL, W_PACKED)),
                  _const_spec((1, LANE)), _const_spec((1, ATTN_WIDTH)), _const_spec((1, ATTN_WIDTH)),
                  row(LANE), row(LANE), row(LANE), row(LANE), _const_spec((ATTN_WIDTH, ATTN_WIDTH))],
        out_specs=out_specs,
        out_shape=out_shape,
        compiler_params=_cparams(("arbitrary",)),
        name="prep",
    )(x, lw["g_mix"], lw["w_in"], lw["gate_bias"], lw["g_q"], lw["g_k"], *pos_tabs, lw["bd64"])
    res = dict(zip(names, outs[:-1]))
    res["gates_t"] = outs[-1]
    return res


def _mlstm_kernel(q_ref, k_ref, v_ref, mo_ref, g_ref, gt_ref, gm_ref, c0_ref, m0_ref,
                  y_ref, cn_ref, mn_ref, c_scr, m_scr):
    j = pl.program_id(1)

    @pl.when(j == 0)
    def _():
        c_scr[...] = c0_ref[0]
        m_scr[...] = m0_ref[0]

    l = q_ref.shape[1]
    g = g_ref[0]
    gt = gt_ref[0]
    r_i = lax.broadcasted_iota(jnp.int32, (l, l), 0)
    c_i = lax.broadcasted_iota(jnp.int32, (l, l), 1)
    causal = c_i <= r_i
    tri_l = jnp.where(causal, 1.0, 0.0).astype(BF16)
    tri_u = jnp.where(r_i <= c_i, 1.0, 0.0).astype(BF16)
    bcum_c = _dot_exact_rhs(tri_l, g)
    bcum_r = _dot_exact_lhs(gt, tri_u)
    lane = lax.broadcasted_iota(jnp.int32, (l, LANE), 1)

    for h in range(M_HEADS):
        sl = slice(h * LANE, (h + 1) * LANE)
        qh, kh, vh = q_ref[0, :, sl], k_ref[0, :, sl], v_ref[0, :, sl]
        bc = bcum_c[:, G_LF + h:G_LF + h + 1]
        br = bcum_r[G_LF + h:G_LF + h + 1, :]
        ig_r = gt[G_IG + h:G_IG + h + 1, :]
        ig_c = g[:, G_IG + h:G_IG + h + 1]
        m_prev = m_scr[h][:, :1]
        c_prev = c_scr[h]

        log_d = jnp.where(causal, bc - br + ig_r, -jnp.inf)
        inter = bc + m_prev
        m_t = jnp.maximum(inter, jnp.max(log_d, axis=1, keepdims=True))
        s = lax.dot_general(qh, kh, _NT, preferred_element_type=F32) * jnp.exp(log_d - m_t)
        inter_w = jnp.exp(inter - m_t)
        num = (jnp.dot(s.astype(BF16), vh, preferred_element_type=F32)
               + inter_w * jnp.dot(qh, c_prev.astype(BF16), preferred_element_type=F32))
        den = num[:, N_COL:N_COL + 1]
        hval = num / jnp.maximum(jnp.abs(den), jnp.exp(-m_t))

        b_last = bc[l - 1:l, :]
        log_w = b_last - bc + ig_c
        m_new = jnp.maximum(b_last + m_prev, jnp.max(log_w, axis=0, keepdims=True))
        w_exp = jnp.exp(log_w - m_new)
        decay = jnp.exp(b_last + m_prev - m_new)
        kw_t = (kh.astype(F32) * w_exp).T.astype(BF16)
        c_scr[h] = decay * c_prev + jnp.dot(kw_t, vh, preferred_element_type=F32)
        m_scr[h] = jnp.broadcast_to(m_new, (1, LANE))

        o = jnp.where(lane < M_DK, jax.nn.sigmoid(mo_ref[0, :, sl]) * hval, 0.0)
        ms = jnp.sum(o * o, axis=1, keepdims=True) * (1.0 / M_DK)
        y_ref[0, :, sl] = (o * lax.rsqrt(ms + EPS) * gm_ref[:, sl]).astype(BF16)

    @pl.when(j == pl.num_programs(1) - 1)
    def _():
        cn_ref[0] = c_scr[...]
        mn_ref[0] = m_scr[...]


def _mlstm_call(q, k, v, mo, gates, gates_t, gm, c0, m0, chunk):
    b, t, _ = q.shape
    tok = lambda w: pl.BlockSpec((1, chunk, w), lambda i, j: (i, j, 0))
    st_c = pl.BlockSpec((1, M_HEADS, LANE, LANE), lambda i, j: (i, 0, 0, 0))
    st_m = pl.BlockSpec((1, M_HEADS, 1, LANE), lambda i, j: (i, 0, 0, 0))
    return pl.pallas_call(
        _mlstm_kernel,
        grid=(b, t // chunk),
        in_specs=[tok(MP), tok(MP), tok(MP), tok(MP), tok(LANE),
                  pl.BlockSpec((1, 8, chunk), lambda i, j: (i, 0, j)),
                  pl.BlockSpec((1, MP), lambda i, j: (0, 0)), st_c, st_m],
        out_specs=[tok(MP), st_c, st_m],
        out_shape=[jax.ShapeDtypeStruct((b, t, MP), BF16),
                   jax.ShapeDtypeStruct((b, M_HEADS, LANE, LANE), F32),
                   jax.ShapeDtypeStruct((b, M_HEADS, 1, LANE), F32)],
        scratch_shapes=[pltpu.VMEM((M_HEADS, LANE, LANE), F32), pltpu.VMEM((M_HEADS, 1, LANE), F32)],
        compiler_params=_cparams(("arbitrary", "arbitrary")),
        name="mlstm",
    )(q, k, v, mo, gates, gates_t, gm, c0, m0)


_HALO = 16


def _pool_kernel(u_ref, hist_ref, w_ref, sc_ref, y_ref, ext, *, pos0, carry):
    j = pl.program_id(1)
    tm = u_ref.shape[1]

    @pl.when(j == 0)
    def _():
        ext[0:_HALO, :] = hist_ref[0]

    ext[_HALO:_HALO + tm, :] = u_ref[0]
    cur = ext[_HALO:_HALO + tm, :]
    pos = pos0 + j * tm + lax.broadcasted_iota(jnp.int32, (tm, 1), 0)
    lane = lax.broadcasted_iota(jnp.int32, (tm, POOL_WIDTH), 1)
    run = cur
    d = None
    back = 1
    for gi, w in enumerate(POOL_WINDOWS):
        while back < w:
            run = run + ext[_HALO - back:_HALO - back + tm, :]
            back += 1
        cnt = jnp.minimum(w, pos + 1).astype(F32)
        dg = run / cnt - cur
        d = dg if d is None else jnp.where(lane >= gi * POOL_GDIM, dg, d)
    y = jnp.dot(d.astype(BF16), w_ref[...], preferred_element_type=F32) * sc_ref[...]
    y_ref[0] = y.astype(BF16)
    if carry:
        ext[0:_HALO, :] = ext[tm:tm + _HALO, :]


def _pool_call(u, hist, w_bd, scale, pos0):
    b, t, _ = u.shape
    tm = min(512, t)
    nt = t // tm
    return pl.pallas_call(
        functools.partial(_pool_kernel, pos0=pos0, carry=nt > 1),
        grid=(b, nt),
        in_specs=[pl.BlockSpec((1, tm, POOL_WIDTH), lambda i, j: (i, j, 0)),
                  pl.BlockSpec((1, _HALO, POOL_WIDTH), lambda i, j: (i, 0, 0)),
                  pl.BlockSpec((POOL_WIDTH, POOL_WIDTH), lambda i, j: (0, 0)),
                  pl.BlockSpec((1, POOL_WIDTH), lambda i, j: (0, 0))],
        out_specs=pl.BlockSpec((1, tm, POOL_WIDTH), lambda i, j: (i, j, 0)),
        out_shape=jax.ShapeDtypeStruct((b, t, POOL_WIDTH), BF16),
        scratch_shapes=[pltpu.VMEM((tm + _HALO, POOL_WIDTH), F32)],
        compiler_params=_cparams(("arbitrary", "arbitrary")),
        name="pool",
    )(u, hist, w_bd, scale)


_FF_CHUNK = 1024


def _finish_kernel(x_ref, mm_ref, ma_ref, mp_ref, wo_ref, g_ref, wu_ref, wd_ref, y_ref):
    o_a, o_p = MP, MP + ATTN_WIDTH
    xr = (x_ref[...]
          + jnp.dot(mm_ref[...], wo_ref[0:o_a, :], preferred_element_type=F32)
          + jnp.dot(ma_ref[...], wo_ref[o_a:o_p, :], preferred_element_type=F32)
          + jnp.dot(mp_ref[...], wo_ref[o_p:o_p + POOL_WIDTH, :], preferred_element_type=F32))
    hn = (xr * lax.rsqrt(jnp.mean(xr * xr, axis=-1, keepdims=True) + EPS) * g_ref[...]).astype(BF16)
    mlp = None
    for c in range(D_FF // _FF_CHUNK):
        sl = slice(c * _FF_CHUNK, (c + 1) * _FF_CHUNK)
        hid = jnp.maximum(jnp.dot(hn, wu_ref[:, sl], preferred_element_type=F32), 0.0)
        d = jnp.dot((hid * hid).astype(BF16), wd_ref[sl, :], preferred_element_type=F32)
        mlp = d if mlp is None else mlp + d
    y_ref[...] = xr + mlp


def _finish_call(x, mix_m, mix_a, mix_p, lw):
    t = x.shape[0]
    tm = min(512, t)
    row = lambda w: pl.BlockSpec((tm, w), lambda i: (i, 0))
    return pl.pallas_call(
        _finish_kernel,
        grid=(t // tm,),
        in_specs=[row(D_MODEL), row(MP), row(ATTN_WIDTH), row(POOL_WIDTH),
                  _const_spec((MP + ATTN_WIDTH + POOL_WIDTH, D_MODEL)), _const_spec((1, D_MODEL)),
                  _const_spec((D_MODEL, D_FF)), _const_spec((D_FF, D_MODEL))],
        out_specs=row(D_MODEL),
        out_shape=jax.ShapeDtypeStruct((t, D_MODEL), F32),
        compiler_params=_cparams(("arbitrary",)),
        name="finish",
    )(x, mix_m, mix_a, mix_p, lw["w_out"], lw["g_mlp"], lw["w_up"], lw["w_down"])


_BIG_SLOTS = 1e9


def _row_max(x):
    return jnp.max(x, axis=1, keepdims=True)


def _select_threshold(count_ge, max_below, mn, mx, n_adm, n_inadm, topk):
    kf = float(topk)
    g0 = n_adm + jnp.where(mn <= NEG, n_inadm, 0.0)
    all_kept = g0 < kf
    top = jnp.maximum(mx, NEG)
    hi0 = top + jnp.abs(top) + 1.0

    def bisect(_, c):
        lo, hi, clo, chi = c
        mid = lo + (hi - lo) * 0.5
        cm = count_ge(mid)
        ge = cm >= kf
        return (jnp.where(ge, mid, lo), jnp.where(ge, hi, mid), jnp.where(ge, cm, clo), jnp.where(ge, chi, cm))

    lo, hi, clo, chi = lax.fori_loop(0, N_BISECT, bisect, (mn, hi0, g0, jnp.zeros_like(mn)))
    thr0 = jnp.where(all_kept, mn, lo)
    done0 = jnp.where(all_kept | (clo == kf), 1.0, 0.0)

    def cond(c):
        return (c[0] <= topk) & (jnp.min(c[5]) < 0.5)

    def peel(c):
        it, (hi, chi, thr, slots, done, tie) = c[0], c[1:]
        v = max_below(hi)
        cv = count_ge(v)
        fin = (cv >= kf) & (done < 0.5)
        thr = jnp.where(fin, v, thr)
        slots = jnp.where(fin, kf - chi, slots)
        tie = jnp.where(fin & (cv > kf), 1.0, tie)
        done = jnp.where(fin, 1.0, done)
        live = done < 0.5
        return (it + 1, jnp.where(live, v, hi), jnp.where(live, cv, chi), thr, slots, done, tie)

    init = (jnp.int32(0), hi, chi, thr0, jnp.full_like(mn, _BIG_SLOTS), done0, jnp.zeros_like(mn))
    _, _, _, thr, slots, _, tie = lax.while_loop(cond, peel, init)
    return thr, slots, jnp.max(tie) > 0.5


def _dsa_prompt_kernel(iq_ref, aq_ref, g_ref, ik_ref, ak_ref, av_ref, o_ref,
                       s_scr, qm_scr, acc_scr, m_scr, l_scr, *, tq, n_total, topk):
    i = pl.program_id(0)
    nkt = i + 1
    tk = tq
    nch = tk // LANE
    q_pos = i * tq + lax.broadcasted_iota(jnp.int32, (tq, 1), 0)
    gates = g_ref[...]
    iw = [gates[:, G_IW + h:G_IW + h + 1] for h in range(IDX_HEADS)]

    def score_tile(kt, carry):
        mx, mn = carry
        kk = ik_ref[pl.ds(pl.multiple_of(kt * tk, tk), tk), :]
        s = jnp.zeros((tq, tk), F32)
        for h in range(IDX_HEADS):
            sc = lax.dot_general(iq_ref[:, h * LANE:(h + 1) * LANE], kk, _NT, preferred_element_type=F32)
            s = s + jnp.maximum(sc, 0.0) * iw[h]
        k_pos = kt * tk + lax.broadcasted_iota(jnp.int32, (1, tk), 1)
        adm = k_pos <= q_pos
        lo_fill = jnp.where(adm, s, -jnp.inf)
        hi_fill = jnp.where(adm, s, jnp.inf)
        s_scr[kt] = lo_fill
        for c in range(nch):
            mx = jnp.maximum(mx, lo_fill[:, c * LANE:(c + 1) * LANE])
            mn = jnp.minimum(mn, hi_fill[:, c * LANE:(c + 1) * LANE])
        return mx, mn

    mx, mn = lax.fori_loop(0, nkt, score_tile,
                           (jnp.full((tq, LANE), -jnp.inf, F32), jnp.full((tq, LANE), jnp.inf, F32)))
    mx = _row_max(mx)
    mn = jnp.min(mn, axis=1, keepdims=True)
    n_adm = (q_pos + 1).astype(F32)
    n_inadm = float(n_total) - n_adm

    def count_ge(t):
        tb = jnp.broadcast_to(t, (tq, LANE))

        def body(kt, acc):
            sv = s_scr[kt]
            for c in range(nch):
                acc = acc + jnp.where(sv[:, c * LANE:(c + 1) * LANE] >= tb, 1.0, 0.0)
            return acc

        acc = lax.fori_loop(0, nkt, body, jnp.zeros((tq, LANE), F32))
        return jnp.sum(acc, axis=1, keepdims=True) + jnp.where(t <= NEG, n_inadm, 0.0)

    def max_below(hi):
        hb = jnp.broadcast_to(hi, (tq, LANE))

        def body(kt, acc):
            sv = s_scr[kt]
            for c in range(nch):
                sc = sv[:, c * LANE:(c + 1) * LANE]
                acc = jnp.maximum(acc, jnp.where(sc < hb, sc, -jnp.inf))
            return acc

        acc = lax.fori_loop(0, nkt, body, jnp.full((tq, LANE), -jnp.inf, F32))
        tail = jnp.where((hi > NEG) & (n_inadm > 0.0), NEG, -jnp.inf)
        return jnp.maximum(_row_max(acc), tail)

    thr, slots, any_tie = _select_threshold(count_ge, max_below, mn, mx, n_adm, n_inadm, topk)

    lane = lax.broadcasted_iota(jnp.int32, (tq, LANE), 1)
    for h in range(A_HEADS):
        qp = aq_ref[:, (h // 2) * LANE:(h // 2 + 1) * LANE].astype(F32)
        keep = (lane < HEAD_DIM) if h % 2 == 0 else (lane >= HEAD_DIM)
        qm_scr[h] = jnp.where(keep, qp, 0.0).astype(BF16)
    acc_scr[...] = jnp.zeros_like(acc_scr)
    l_scr[...] = jnp.zeros_like(l_scr)
    m_scr[...] = jnp.full_like(m_scr, MASK_VALUE)
    thr_b = jnp.broadcast_to(thr, (tq, LANE))

    def attend(kt, kept):
        rows = pl.ds(pl.multiple_of(kt * tk, tk), tk)
        for p in range(A_HEADS // 2):
            kp = ak_ref[rows, p * LANE:(p + 1) * LANE]
            vp = av_ref[rows, p * LANE:(p + 1) * LANE]
            for h in (2 * p, 2 * p + 1):
                lg = lax.dot_general(qm_scr[h], kp, _NT, preferred_element_type=F32)
                lgm = [jnp.where(kept[c], lg[:, c * LANE:(c + 1) * LANE], MASK_VALUE) for c in range(nch)]
                cm = lgm[0]
                for c in range(1, nch):
                    cm = jnp.maximum(cm, lgm[c])
                m_old = m_scr[h]
                m_new = jnp.maximum(m_old, _row_max(cm))
                alpha = jnp.exp(m_old - m_new)
                pc = [jnp.exp(x - m_new) for x in lgm]
                l_scr[h] = alpha * l_scr[h] + sum(pc)
                pmat = jnp.concatenate(pc, axis=1).astype(BF16)
                acc_scr[h] = alpha * acc_scr[h] + jnp.dot(pmat, vp, preferred_element_type=F32)
                m_scr[h] = m_new

    def fast_path():
        def body(kt, _):
            sv = s_scr[kt]
            attend(kt, [sv[:, c * LANE:(c + 1) * LANE] >= thr_b for c in range(nch)])
            return 0
        lax.fori_loop(0, nkt, body, 0)

    def tie_path():
        r_i = lax.broadcasted_iota(jnp.int32, (tk, tk), 0)
        c_i = lax.broadcasted_iota(jnp.int32, (tk, tk), 1)
        before = jnp.where(r_i < c_i, 1.0, 0.0).astype(BF16)
        slots_b = jnp.broadcast_to(slots, (tq, tk))
        thr_t = jnp.broadcast_to(thr, (tq, tk))

        def body(kt, seen):
            sv = s_scr[kt]
            eq = jnp.where(sv == thr_t, 1.0, 0.0)
            prior = jnp.dot(eq.astype(BF16), before, preferred_element_type=F32) + seen
            kept_f = jnp.where(sv > thr_t, 1.0, jnp.where(prior < slots_b, eq, 0.0))
            attend(kt, [kept_f[:, c * LANE:(c + 1) * LANE] > 0.5 for c in range(nch)])
            return seen + jnp.sum(eq, axis=1, keepdims=True)
        lax.fori_loop(0, nkt, body, jnp.zeros((tq, 1), F32))

    lax.cond(any_tie, tie_path, fast_path)

    for p in range(A_HEADS // 2):
        o0 = acc_scr[2 * p] / jnp.sum(l_scr[2 * p], axis=1, keepdims=True)
        o1 = acc_scr[2 * p + 1] / jnp.sum(l_scr[2 * p + 1], axis=1, keepdims=True)
        o_ref[:, p * LANE:(p + 1) * LANE] = jnp.where(lane < HEAD_DIM, o0, o1).astype(BF16)


def _dsa_prompt_call(iq, aq, gates, ikb, akb, avb):
    t = iq.shape[0]
    tq = min(256, t)
    topk = min(TOPK_MAX, t // 4)
    row = lambda w: pl.BlockSpec((tq, w), lambda i: (i, 0))
    return pl.pallas_call(
        functools.partial(_dsa_prompt_kernel, tq=tq, n_total=t, topk=topk),
        grid=(t // tq,),
        in_specs=[row(IDX_HEADS * LANE), row(ATTN_WIDTH), row(LANE),
                  _const_spec((t, LANE)), _const_spec((t, ATTN_WIDTH)), _const_spec((t, ATTN_WIDTH))],
        out_specs=row(ATTN_WIDTH),
        out_shape=jax.ShapeDtypeStruct((t, ATTN_WIDTH), BF16),
        scratch_shapes=[pltpu.VMEM((t // tq, tq, tq), F32),
                        pltpu.VMEM((A_HEADS, tq, LANE), BF16),
                        pltpu.VMEM((A_HEADS, tq, LANE), F32),
                        pltpu.VMEM((A_HEADS, tq, LANE), F32),
                        pltpu.VMEM((A_HEADS, tq, LANE), F32)],
        compiler_params=_cparams(("arbitrary",)),
        name="dsa_prompt",
    )(iq, aq, gates, ikb, akb, avb)


ROWS_T = 8
PAGES_PER_DMA = 8


def _dsa_sample_kernel(pt_ref, iq_ref, iw_ref, aq_ref, ikn_ref, kn_ref, vn_ref, ci_ref, ck_ref, cv_ref, o_ref,
                       s_scr, ibuf, kbuf, vbuf, isem, ksem, vsem, acc_scr, m_scr, l_scr,
                       *, layer, n_pages, n_new, past_len, topk):
    b = pl.program_id(0)
    page = LANE
    pg = PAGES_PER_DMA
    n_chunks = n_pages // pg
    a_rows = A_HEADS * ROWS_T

    def idx_copies(c, slot):
        return [pltpu.make_async_copy(ci_ref.at[layer, pt_ref[b, c * pg + p]], ibuf.at[slot, p], isem.at[slot])
                for p in range(pg)]

    def kv_copies(c, slot):
        cps = []
        for p in range(pg):
            phys = pt_ref[b, c * pg + p]
            cps.append(pltpu.make_async_copy(ck_ref.at[layer, phys], kbuf.at[slot, p], ksem.at[slot]))
            cps.append(pltpu.make_async_copy(cv_ref.at[layer, phys], vbuf.at[slot, p], vsem.at[slot]))
        return cps

    tok = lax.broadcasted_iota(jnp.int32, (ROWS_T, 1), 0)
    slot_i = lax.broadcasted_iota(jnp.int32, (1, page), 1)
    new_adm = (slot_i <= tok) & (slot_i < n_new)
    iq = iq_ref[0]
    iw = iw_ref[0]

    def page_scores(page_f32):
        sc = jnp.dot(iq, page_f32.astype(BF16), preferred_element_type=F32)
        r = jnp.maximum(sc, 0.0) * iw
        s = r[0:ROWS_T]
        for h in range(1, IDX_HEADS):
            s = s + r[h * ROWS_T:(h + 1) * ROWS_T]
        return s

    for cp in idx_copies(0, 0):
        cp.start()

    def score_chunk(c, carry):
        mx, mn = carry
        slot = c % 2

        @pl.when(c + 1 < n_chunks)
        def _():
            for cp in idx_copies(c + 1, 1 - slot):
                cp.start()

        for cp in idx_copies(c, slot):
            cp.wait()
        for p in range(pg):
            s = page_scores(ibuf[slot, p])
            s_scr[c * pg + p] = s
            mx = jnp.maximum(mx, s)
            mn = jnp.minimum(mn, s)
        return mx, mn

    mx, mn = lax.fori_loop(0, n_chunks, score_chunk,
                           (jnp.full((ROWS_T, page), -jnp.inf, F32), jnp.full((ROWS_T, page), jnp.inf, F32)))
    s_new = page_scores(ikn_ref[0])
    s_scr[n_pages] = jnp.where(new_adm, s_new, -jnp.inf)
    mx = _row_max(jnp.maximum(mx, jnp.where(new_adm, s_new, -jnp.inf)))
    mn = jnp.min(jnp.minimum(mn, jnp.where(new_adm, s_new, jnp.inf)), axis=1, keepdims=True)

    real = tok < n_new
    n_adm = jnp.where(real, (past_len + tok + 1).astype(F32), 0.0)
    n_inadm = jnp.where(real, (n_new - 1 - tok).astype(F32), 0.0)

    def count_ge(t):
        tb = jnp.broadcast_to(t, (ROWS_T, page))

        def body(j, acc):
            return acc + jnp.where(s_scr[j] >= tb, 1.0, 0.0)

        acc = lax.fori_loop(0, n_pages + 1, body, jnp.zeros((ROWS_T, page), F32))
        return jnp.sum(acc, axis=1, keepdims=True) + jnp.where(t <= NEG, n_inadm, 0.0)

    def max_below(hi):
        hb = jnp.broadcast_to(hi, (ROWS_T, page))

        def body(j, acc):
            sv = s_scr[j]
            return jnp.maximum(acc, jnp.where(sv < hb, sv, -jnp.inf))

        acc = lax.fori_loop(0, n_pages + 1, body, jnp.full((ROWS_T, page), -jnp.inf, F32))
        tail = jnp.where((hi > NEG) & (n_inadm > 0.0), NEG, -jnp.inf)
        return jnp.maximum(_row_max(acc), tail)

    thr, slots, any_tie = _select_threshold(count_ge, max_below, mn, mx, n_adm, n_inadm, topk)

    acc_scr[...] = jnp.zeros_like(acc_scr)
    l_scr[...] = jnp.zeros_like(l_scr)
    m_scr[...] = jnp.full_like(m_scr, MASK_VALUE)
    qbd = aq_ref[0]
    thr_b = jnp.broadcast_to(thr, (ROWS_T, page))
    slots_b = jnp.broadcast_to(slots, (ROWS_T, page))
    r_i = lax.broadcasted_iota(jnp.int32, (page, page), 0)
    c_i = lax.broadcasted_iota(jnp.int32, (page, page), 1)
    before = jnp.where(r_i < c_i, 1.0, 0.0).astype(BF16)

    def kept_mask(j, seen):
        sv = s_scr[j]
        eq = jnp.where(sv == thr_b, 1.0, 0.0)
        prior = jnp.dot(eq.astype(BF16), before, preferred_element_type=F32) + seen
        kept = jnp.where(sv > thr_b, 1.0, jnp.where(prior < slots_b, eq, 0.0))
        return kept, seen + jnp.sum(eq, axis=1, keepdims=True)

    def attend(k_pages, v_pages, kept):
        lgm = []
        for kt, km in zip(k_pages, kept):
            lg = jnp.dot(qbd, kt.astype(BF16), preferred_element_type=F32)
            km_all = jnp.concatenate([km] * A_HEADS, axis=0)
            lgm.append(jnp.where(km_all > 0.5, lg, MASK_VALUE))
        cm = lgm[0]
        for x in lgm[1:]:
            cm = jnp.maximum(cm, x)
        m_old = m_scr[...]
        m_new = jnp.maximum(m_old, _row_max(cm))
        alpha = jnp.exp(m_old - m_new)
        pc = [jnp.exp(x - m_new) for x in lgm]
        l_scr[...] = alpha * l_scr[...] + sum(pc)
        pv = None
        for p_t, vt in zip(pc, v_pages):
            d = lax.dot_general(p_t.astype(BF16), vt.astype(BF16), _NT, preferred_element_type=F32)
            pv = d if pv is None else pv + d
        acc_scr[...] = alpha[:, :1] * acc_scr[...] + pv
        m_scr[...] = m_new

    del any_tie
    for cp in kv_copies(0, 0):
        cp.start()

    def chunk(c, seen):
        slot = c % 2

        @pl.when(c + 1 < n_chunks)
        def _():
            for cp in kv_copies(c + 1, 1 - slot):
                cp.start()

        for cp in kv_copies(c, slot):
            cp.wait()
        kept = []
        for p in range(pg):
            km, seen = kept_mask(c * pg + p, seen)
            kept.append(km)
        attend([kbuf[slot, p] for p in range(pg)], [vbuf[slot, p] for p in range(pg)], kept)
        return seen

    seen = lax.fori_loop(0, n_chunks, chunk, jnp.zeros((ROWS_T, 1), F32))
    km, _ = kept_mask(n_pages, seen)
    attend([kn_ref[0]], [vn_ref[0]], [km])

    out_rows = acc_scr[...] / jnp.sum(l_scr[...], axis=1, keepdims=True)
    lane = lax.broadcasted_iota(jnp.int32, (ROWS_T, ATTN_WIDTH), 1)
    out = jnp.zeros((ROWS_T, ATTN_WIDTH), F32)
    for h in range(A_HEADS):
        out = jnp.where(lane // HEAD_DIM == h, out_rows[h * ROWS_T:(h + 1) * ROWS_T], out)
    o_ref[0] = out.astype(BF16)


def _dsa_sample_call(layer, page_table, iq, iw, aq, ik_new, k_new, v_new, ci, ck, cv, n_new):
    b, n_pages = page_table.shape
    page = ci.shape[-1]
    assert page == LANE and n_pages % PAGES_PER_DMA == 0
    past_len = n_pages * page
    topk = min(TOPK_MAX, (past_len + n_new) // 4)
    a_rows = A_HEADS * ROWS_T
    bspec = lambda r, w: pl.BlockSpec((1, r, w), lambda i, pt: (i, 0, 0))
    hbm = pl.BlockSpec(memory_space=pl.ANY)
    pg = PAGES_PER_DMA
    return pl.pallas_call(
        functools.partial(_dsa_sample_kernel, layer=layer, n_pages=n_pages, n_new=n_new, past_len=past_len, topk=topk),
        grid_spec=pltpu.PrefetchScalarGridSpec(
            num_scalar_prefetch=1,
            grid=(b,),
            in_specs=[bspec(IDX_HEADS * ROWS_T, IDX_DIM), bspec(IDX_HEADS * ROWS_T, LANE), bspec(a_rows, ATTN_WIDTH),
                      bspec(IDX_DIM, page), bspec(ATTN_WIDTH, page), bspec(ATTN_WIDTH, page), hbm, hbm, hbm],
            out_specs=bspec(ROWS_T, ATTN_WIDTH),
            scratch_shapes=[pltpu.VMEM((n_pages + 1, ROWS_T, page), F32),
                            pltpu.VMEM((2, pg, IDX_DIM, page), F32),
                            pltpu.VMEM((2, pg, ATTN_WIDTH, page), F32),
                            pltpu.VMEM((2, pg, ATTN_WIDTH, page), F32),
                            pltpu.SemaphoreType.DMA((2,)), pltpu.SemaphoreType.DMA((2,)), pltpu.SemaphoreType.DMA((2,)),
                            pltpu.VMEM((a_rows, ATTN_WIDTH), F32),
                            pltpu.VMEM((a_rows, page), F32),
                            pltpu.VMEM((a_rows, page), F32)]),
        out_shape=jax.ShapeDtypeStruct((b, ROWS_T, ATTN_WIDTH), BF16),
        compiler_params=_cparams(("arbitrary",)),
        name="dsa_sample",
    )(page_table, iq, iw, aq, ik_new, k_new, v_new, ci, ck, cv)


def _pad_heads(a, n_heads, head_dim):
    r = a.shape[0]
    a = a.reshape(r, n_heads, head_dim)
    return jnp.pad(a, ((0, 0), (0, 0), (0, LANE - head_dim))).reshape(r, n_heads * LANE)


def _pack_layer(l, g_mix, w_in, b_i, b_f, g_q, g_k, g_mhead, w_pool, pool_scale, w_out, g_mlp, w_up, w_down):
    mq, mk, mv, mo, mi, mf, aq, ak, av, iq, ik, iw, pu = jnp.split(w_in[l], SPLIT_POINTS, axis=1)
    gate_w = jnp.pad(jnp.concatenate([mi, mf, iw], axis=1), ((0, 0), (0, LANE - 3 * M_HEADS)))
    w_packed = jnp.concatenate(
        [_pad_heads(mq, M_HEADS, M_DK), _pad_heads(mk, M_HEADS, M_DK), _pad_heads(mv, M_HEADS, M_DK),
         _pad_heads(mo, M_HEADS, M_DK), aq, ak, av, _pad_heads(iq, IDX_HEADS, IDX_DIM),
         jnp.pad(ik, ((0, 0), (0, LANE - IDX_DIM))), pu, gate_w], axis=1).astype(BF16)
    gate_bias = jnp.pad(jnp.concatenate([b_i[l], b_f[l]]), (0, LANE - 2 * M_HEADS)).reshape(1, LANE)
    head_id = np.arange(ATTN_WIDTH) // HEAD_DIM
    bd64 = jnp.asarray(head_id[:, None] == head_id[None, :], BF16)
    grp = np.arange(POOL_WIDTH) // POOL_GDIM
    w_bd = jnp.where(jnp.asarray(grp[:, None] == grp[None, :]),
                     jnp.tile(w_pool[l].reshape(POOL_WIDTH, POOL_GDIM), (1, len(POOL_WINDOWS))), 0.0).astype(BF16)
    wo = w_out[l]
    w_out_p = jnp.concatenate([_pad_heads(wo[:MLSTM_WIDTH].T, M_HEADS, M_DK).T, wo[MLSTM_WIDTH:]], axis=0).astype(BF16)
    return dict(
        g_mix=g_mix[l].reshape(1, D_MODEL), w_in=w_packed, gate_bias=gate_bias,
        g_q=jnp.tile(g_q[l], A_HEADS).reshape(1, ATTN_WIDTH), g_k=jnp.tile(g_k[l], A_HEADS).reshape(1, ATTN_WIDTH),
        bd64=bd64, g_mhead=_pad_heads(g_mhead[l].reshape(1, MLSTM_WIDTH), M_HEADS, M_DK),
        w_pool=w_bd, pool_scale=pool_scale[l].reshape(1, POOL_WIDTH),
        w_out=w_out_p, g_mlp=g_mlp[l].reshape(1, D_MODEL), w_up=w_up[l].astype(BF16), w_down=w_down[l].astype(BF16))


def _rope_tables(pos):
    t = pos.shape[0]
    half = ROPE_DIMS // 2
    inv = ROPE_THETA ** (-jnp.arange(half, dtype=F32) * 2.0 / ROPE_DIMS)
    ang = pos.astype(F32)[:, None] * inv[None, :]
    cos, sin = jnp.cos(ang), jnp.sin(ang)
    rest = HEAD_DIM - ROPE_DIMS
    blk_c = jnp.concatenate([cos, cos, jnp.ones((t, rest), F32)], axis=1)
    blk_s = jnp.concatenate([-sin, sin, jnp.zeros((t, rest), F32)], axis=1)
    pad_c, pad_s = jnp.ones((t, LANE - HEAD_DIM), F32), jnp.zeros((t, LANE - HEAD_DIM), F32)
    return (jnp.concatenate([blk_c, blk_c], axis=1), jnp.concatenate([blk_s, blk_s], axis=1),
            jnp.concatenate([blk_c, pad_c], axis=1), jnp.concatenate([blk_s, pad_s], axis=1))


def _unpad_state(c_aug, m):
    return c_aug[:, :, :M_DK, :M_DK], c_aug[:, :, :M_DK, N_COL], m[:, :, 0, 0]


def _prompt_layer(x, tabs, lw):
    t = x.shape[0]
    pr = _prep_call(x, tabs, lw)
    chunk = min(256, t)
    add_b = lambda a: a[None]
    y_m, c_n, m_n = _mlstm_call(add_b(pr["mq"]), add_b(pr["mk"]), add_b(pr["mv"]), add_b(pr["mo"]),
                                add_b(pr["gates"]), add_b(pr["gates_t"]), lw["g_mhead"],
                                jnp.zeros((1, M_HEADS, LANE, LANE), F32), jnp.zeros((1, M_HEADS, 1, LANE), F32), chunk)
    y_p = _pool_call(add_b(pr["pu"]), jnp.zeros((1, _HALO, POOL_WIDTH), F32), lw["w_pool"], lw["pool_scale"], 0)
    y_a = _dsa_prompt_call(pr["iq"], pr["aq"], pr["gates"], pr["ikb"], pr["akb"], pr["avb"])
    x_new = _finish_call(x, y_m[0], y_a, y_p[0], lw)
    return (x_new, pr["ak32"].reshape(1, t, A_HEADS, HEAD_DIM), pr["av32"].reshape(1, t, A_HEADS, HEAD_DIM),
            pr["ik32"].reshape(1, t, IDX_DIM), _unpad_state(c_n, m_n), pr["pu"][None, t - POOL_BUF:])


S_CHUNK = 128


def _sample_layer(layer, x, tabs, lw, st_c, st_n, st_m, st_pool, page_table, ci, ck, cv):
    b, t, _ = x.shape
    x2 = x.reshape(b * t, D_MODEL)
    pr = _prep_call(x2, tabs, lw)
    per = lambda a: a.reshape(b, t, a.shape[-1])
    pad_t = lambda a, n: jnp.pad(a, ((0, 0), (0, n - t), (0, 0)))

    gates = per(pr["gates"])
    pad_row = jnp.where(jnp.arange(LANE) < M_HEADS, NEG, 0.0).astype(F32)
    gates_p = jnp.concatenate([gates, jnp.broadcast_to(pad_row, (b, S_CHUNK - t, LANE))], axis=1)
    gates_t = jnp.swapaxes(gates_p[:, :, :8], 1, 2)
    c0 = jnp.pad(st_c, ((0, 0), (0, 0), (0, LANE - M_DK), (0, LANE - M_DK))).at[:, :, :M_DK, N_COL].set(st_n)
    m0 = jnp.broadcast_to(st_m[:, :, None, None], (b, M_HEADS, 1, LANE))
    y_m, c_n, m_n = _mlstm_call(pad_t(per(pr["mq"]), S_CHUNK), pad_t(per(pr["mk"]), S_CHUNK),
                                pad_t(per(pr["mv"]), S_CHUNK), pad_t(per(pr["mo"]), S_CHUNK),
                                gates_p, gates_t, lw["g_mhead"], c0, m0, S_CHUNK)

    past_len = page_table.shape[1] * ci.shape[-1]
    hist = jnp.concatenate([jnp.zeros((b, _HALO - POOL_BUF, POOL_WIDTH), F32), st_pool], axis=1)
    y_p = _pool_call(pad_t(per(pr["pu"]), ROWS_T), hist, lw["w_pool"], lw["pool_scale"], past_len)

    heads_first = lambda a: jnp.pad(jnp.swapaxes(a, 1, 2), ((0, 0), (0, 0), (0, ROWS_T - t), (0, 0)))
    iq = heads_first(per(pr["iq"]).reshape(b, t, IDX_HEADS, LANE)[..., :IDX_DIM])
    iq = iq.reshape(b, IDX_HEADS * ROWS_T, IDX_DIM)
    iw = heads_first(gates[:, :, G_IW:G_IW + IDX_HEADS, None]).reshape(b, IDX_HEADS * ROWS_T, 1)
    iw = jnp.broadcast_to(iw, (b, IDX_HEADS * ROWS_T, LANE))
    aq = heads_first(per(pr["aq"]).reshape(b, t, A_HEADS, HEAD_DIM))
    qbd = aq[:, :, :, None, :] * jnp.eye(A_HEADS, dtype=BF16)[None, :, None, :, None]
    qbd = qbd.reshape(b, A_HEADS * ROWS_T, ATTN_WIDTH)
    to_page = lambda a: jnp.pad(jnp.swapaxes(per(a), 1, 2), ((0, 0), (0, 0), (0, LANE - t)))
    y_a = _dsa_sample_call(layer, page_table, iq, iw, qbd, to_page(pr["ik32"]), to_page(pr["ak32"]),
                           to_page(pr["av32"]), ci, ck, cv, t)

    x_new = _finish_call(x2, y_m[:, :t].reshape(b * t, MP), y_a[:, :t].reshape(b * t, ATTN_WIDTH),
                         y_p[:, :t].reshape(b * t, POOL_WIDTH), lw)
    pool_buf = jnp.concatenate([st_pool, per(pr["pu"])], axis=1)[:, -POOL_BUF:]
    return (x_new.reshape(b, t, D_MODEL), pr["ak32"].reshape(b, t, A_HEADS, HEAD_DIM),
            pr["av32"].reshape(b, t, A_HEADS, HEAD_DIM), per(pr["ik32"]), _unpad_state(c_n, m_n), pool_buf)


def kernel(x_prompt, x_sample, cache_k, cache_v, cache_idx_k, state_C, state_n, state_m, state_pool, page_table,
           g_mix, w_in, b_i, b_f, g_q, g_k, g_mhead, w_pool, pool_scale, w_out, g_mlp, w_up, w_down):
    depth = w_in.shape[0]
    bp, s, _ = x_prompt.shape
    db, t, _ = x_sample.shape
    assert bp == 1 and t <= ROWS_T
    n_pool, page = cache_k.shape[1], cache_k.shape[2]
    past_len = page_table.shape[1] * page
    tabs_p = _rope_tables(jnp.arange(s, dtype=jnp.int32))
    tabs_s = _rope_tables(jnp.tile(past_len + jnp.arange(t, dtype=jnp.int32), db))
    ck = jnp.transpose(cache_k, (0, 1, 3, 4, 2)).reshape(depth, n_pool, ATTN_WIDTH, page)
    cv = jnp.transpose(cache_v, (0, 1, 3, 4, 2)).reshape(depth, n_pool, ATTN_WIDTH, page)
    ci = jnp.transpose(cache_idx_k, (0, 1, 3, 2))

    xp, xs = x_prompt[0], x_sample
    outs_p, outs_s = [], []
    for l in range(depth):
        lw = _pack_layer(l, g_mix, w_in, b_i, b_f, g_q, g_k, g_mhead, w_pool, pool_scale, w_out, g_mlp, w_up, w_down)
        xp, *rest_p = _prompt_layer(xp, tabs_p, lw)
        outs_p.append(rest_p)
        xs, *rest_s = _sample_layer(l, xs, tabs_s, lw, state_C[l], state_n[l], state_m[l], state_pool[l],
                                    page_table, ci, ck, cv)
        outs_s.append(rest_s)

    def stack(outs):
        k, v, ik, st, pool = zip(*outs)
        c, n, m = zip(*st)
        return [jnp.stack(a, axis=0) for a in (k, v, ik, c, n, m, pool)]

    return (xp[None], xs, *stack(outs_p), *stack(outs_s))
```

```python
import functools

import numpy as np
import jax
import jax.numpy as jnp
from jax import lax
from jax.experimental import pallas as pl
from jax.experimental.pallas import tpu as pltpu

F32 = jnp.float32
BF16 = jnp.bfloat16

D_MODEL = 1024
M_HEADS = 4
M_DK = 96
A_HEADS = 6
HEAD_DIM = 64
ATTN_WIDTH = A_HEADS * HEAD_DIM
IDX_HEADS = 4
IDX_DIM = 64
TOPK_MAX = 256
ROPE_THETA = 500000.0
ROPE_DIMS = HEAD_DIM // 4
POOL_WINDOWS = (2, 4, 8, 16)
POOL_GDIM = 64
POOL_WIDTH = 256
POOL_BUF = 15
D_FF = 4 * D_MODEL
EPS = 1e-6
NEG = -1e30
MLSTM_WIDTH = M_HEADS * M_DK
SPLIT_SIZES = (MLSTM_WIDTH, MLSTM_WIDTH, MLSTM_WIDTH, MLSTM_WIDTH, M_HEADS, M_HEADS,
               ATTN_WIDTH, ATTN_WIDTH, ATTN_WIDTH, IDX_HEADS * IDX_DIM, IDX_DIM, IDX_HEADS, POOL_WIDTH)
SPLIT_POINTS = tuple(int(s) for s in np.cumsum(SPLIT_SIZES)[:-1])

LANE = 128
MP = M_HEADS * LANE
N_COL = M_DK
OFF_MQ, OFF_MK, OFF_MV, OFF_MO = 0, MP, 2 * MP, 3 * MP
OFF_AQ = 4 * MP
OFF_AK = OFF_AQ + ATTN_WIDTH
OFF_AV = OFF_AK + ATTN_WIDTH
OFF_IQ = OFF_AV + ATTN_WIDTH
OFF_IK = OFF_IQ + IDX_HEADS * LANE
OFF_PU = OFF_IK + LANE
OFF_G = OFF_PU + POOL_WIDTH
W_PACKED = OFF_G + LANE
G_IG, G_LF, G_IW = 0, M_HEADS, 2 * M_HEADS

VMEM_LIMIT = 56 * 1024 * 1024
MAX_SEARCH = 48
MASK_VALUE = -1e30
MAX_SHIFT_BOUND = 40.0
BOUND_SLACK = 1.02

_NT = (((1,), (1,)), ((), ()))


def _cparams(sem):
    return pltpu.CompilerParams(dimension_semantics=sem, vmem_limit_bytes=VMEM_LIMIT)


def _const_spec(shape):
    nd = len(shape)
    return pl.BlockSpec(shape, lambda *_: (0,) * nd, pipeline_mode=pl.Buffered(1))


def _split3(a):
    a1 = a.astype(BF16)
    r1 = a - a1.astype(F32)
    a2 = r1.astype(BF16)
    a3 = (r1 - a2.astype(F32)).astype(BF16)
    return a1, a2, a3


def _dot_exact_lhs(a, b01):
    return sum(jnp.dot(t, b01, preferred_element_type=F32) for t in _split3(a))


def _dot_exact_rhs(a01, b):
    return sum(jnp.dot(a01, t, preferred_element_type=F32) for t in _split3(b))


def _rope_group(x, c, s, period):
    lane = lax.broadcasted_iota(jnp.int32, x.shape, 1)
    first_half = (lane % period) < (ROPE_DIMS // 2)
    partner = jnp.where(first_half, pltpu.roll(x, LANE - ROPE_DIMS // 2, 1), pltpu.roll(x, ROPE_DIMS // 2, 1))
    return x * c + partner * s


def _rope(x, c, s, period):
    groups = [_rope_group(x[:, g * LANE:(g + 1) * LANE], c, s, period) for g in range(x.shape[1] // LANE)]
    return groups[0] if len(groups) == 1 else jnp.concatenate(groups, axis=1)


def _prep_kernel(x_ref, g_ref, w_ref, bias_ref, gq_ref, gk_ref, c64_ref, s64_ref, c128_ref, s128_ref, bd_ref,
                 mq_o, mk_o, mv_o, mo_o, aq_o, ak32_o, akb_o, av32_o, avb_o, iq_o, ik32_o, ikb_o, pu_o,
                 gates_o, gates_t_o):
    x = x_ref[...]
    h = (x * lax.rsqrt(jnp.mean(x * x, axis=-1, keepdims=True) + EPS) * g_ref[...]).astype(BF16)

    def seg(off, width):
        return jnp.dot(h, w_ref[:, off:off + width], preferred_element_type=F32)

    tm = x.shape[0]
    lane_mp = lax.broadcasted_iota(jnp.int32, (tm, MP), 1)
    mq_o[...] = seg(OFF_MQ, MP).astype(BF16)
    mk_o[...] = (seg(OFF_MK, MP) * (M_DK ** -0.5)).astype(BF16)
    mv_o[...] = jnp.where(lane_mp % LANE == N_COL, 1.0, seg(OFF_MV, MP)).astype(BF16)
    mo_o[...] = seg(OFF_MO, MP)

    c64, s64, c128, s128 = c64_ref[...], s64_ref[...], c128_ref[...], s128_ref[...]
    bd = bd_ref[...]

    def qk_norm(z, gain):
        ms = _dot_exact_lhs(z * z, bd) * (1.0 / HEAD_DIM)
        return _rope(z * lax.rsqrt(ms + EPS) * gain, c64, s64, HEAD_DIM)

    aq = qk_norm(seg(OFF_AQ, ATTN_WIDTH), gq_ref[...])
    aq_o[...] = (aq * (HEAD_DIM ** -0.5)).astype(BF16)
    ak = qk_norm(seg(OFF_AK, ATTN_WIDTH), gk_ref[...])
    ak32_o[...] = ak
    akb_o[...] = ak.astype(BF16)
    av = seg(OFF_AV, ATTN_WIDTH)
    av32_o[...] = av
    avb_o[...] = av.astype(BF16)
    iq_o[...] = _rope(seg(OFF_IQ, IDX_HEADS * LANE), c128, s128, LANE).astype(BF16)
    ik = _rope(seg(OFF_IK, LANE), c128, s128, LANE)
    ik32_o[...] = ik[:, :IDX_DIM]
    ikb_o[...] = ik.astype(BF16)
    pu_o[...] = seg(OFF_PU, POOL_WIDTH)

    zg = seg(OFF_G, LANE)
    zb = zg + bias_ref[...]
    lane = lax.broadcasted_iota(jnp.int32, zg.shape, 1)
    log_sig = jnp.minimum(zb, 0.0) - jnp.log1p(jnp.exp(-jnp.abs(zb)))
    iw_scale = (IDX_HEADS ** -0.5) * (IDX_DIM ** -0.5)
    gates = jnp.where(lane < G_LF, zb,
                      jnp.where(lane < G_IW, log_sig,
                                jnp.where(lane < G_IW + IDX_HEADS, zg * iw_scale, 0.0)))
    gates_o[...] = gates
    gates_t_o[...] = gates.T[:8, :]


def _prep_call(x, pos_tabs, lw):
    t = x.shape[0]
    tm = min(256, t)
    row = lambda w: pl.BlockSpec((tm, w), lambda i: (i, 0))
    out_shapes = dict(
        mq=(MP, BF16), mk=(MP, BF16), mv=(MP, BF16), mo=(MP, F32),
        aq=(ATTN_WIDTH, BF16), ak32=(ATTN_WIDTH, F32), akb=(ATTN_WIDTH, BF16),
        av32=(ATTN_WIDTH, F32), avb=(ATTN_WIDTH, BF16),
        iq=(IDX_HEADS * LANE, BF16), ik32=(IDX_DIM, F32), ikb=(LANE, BF16),
        pu=(POOL_WIDTH, F32), gates=(LANE, F32))
    names = list(out_shapes)
    out_shape = [jax.ShapeDtypeStruct((t, w), dt) for w, dt in out_shapes.values()]
    out_specs = [row(w) for w, _ in out_shapes.values()]
    out_shape.append(jax.ShapeDtypeStruct((8, t), F32))
    out_specs.append(pl.BlockSpec((8, tm), lambda i: (0, i)))
    outs = pl.pallas_call(
        _prep_kernel,
        grid=(t // tm,),
        in_specs=[row(D_MODEL), _const_spec((1, D_MODEL)), _const_spec((D_MODEL, W_PACKED)),
                  _const_spec((1, LANE)), _const_spec((1, ATTN_WIDTH)), _const_spec((1, ATTN_WIDTH)),
                  row(LANE), row(LANE), row(LANE), row(LANE), _const_spec((ATTN_WIDTH, ATTN_WIDTH))],
        out_specs=out_specs,
        out_shape=out_shape,
        compiler_params=_cparams(("arbitrary",)),
        name="prep",
    )(x, lw["g_mix"], lw["w_in"], lw["gate_bias"], lw["g_q"], lw["g_k"], *pos_tabs, lw["bd64"])
    res = dict(zip(names, outs[:-1]))
    res["gates_t"] = outs[-1]
    return res


def _mlstm_kernel(q_ref, k_ref, v_ref, mo_ref, g_ref, gt_ref, gm_ref, c0_ref, m0_ref,
                  y_ref, cn_ref, mn_ref, c_scr, m_scr):
    j = pl.program_id(1)

    @pl.when(j == 0)
    def _():
        c_scr[...] = c0_ref[0]
        m_scr[...] = m0_ref[0]

    l = q_ref.shape[1]
    g = g_ref[0]
    gt = gt_ref[0]
    r_i = lax.broadcasted_iota(jnp.int32, (l, l), 0)
    c_i = lax.broadcasted_iota(jnp.int32, (l, l), 1)
    causal = c_i <= r_i
    tri_l = jnp.where(causal, 1.0, 0.0).astype(BF16)
    tri_u = jnp.where(r_i <= c_i, 1.0, 0.0).astype(BF16)
    bcum_c = _dot_exact_rhs(tri_l, g)
    bcum_r = _dot_exact_lhs(gt, tri_u)
    lane = lax.broadcasted_iota(jnp.int32, (l, LANE), 1)

    for h in range(M_HEADS):
        sl = slice(h * LANE, (h + 1) * LANE)
        qh, kh, vh = q_ref[0, :, sl], k_ref[0, :, sl], v_ref[0, :, sl]
        bc = bcum_c[:, G_LF + h:G_LF + h + 1]
        br = bcum_r[G_LF + h:G_LF + h + 1, :]
        ig_r = gt[G_IG + h:G_IG + h + 1, :]
        ig_c = g[:, G_IG + h:G_IG + h + 1]
        m_prev = m_scr[h][:, :1]
        c_prev = c_scr[h]

        log_d = jnp.where(causal, bc - br + ig_r, -jnp.inf)
        inter = bc + m_prev
        m_t = jnp.maximum(inter, jnp.max(log_d, axis=1, keepdims=True))
        s = lax.dot_general(qh, kh, _NT, preferred_element_type=F32) * jnp.exp(log_d - m_t)
        inter_w = jnp.exp(inter - m_t)
        num = (jnp.dot(s.astype(BF16), vh, preferred_element_type=F32)
               + inter_w * jnp.dot(qh, c_prev.astype(BF16), preferred_element_type=F32))
        den = num[:, N_COL:N_COL + 1]
        hval = num / jnp.maximum(jnp.abs(den), jnp.exp(-m_t))

        b_last = bc[l - 1:l, :]
        log_w = b_last - bc + ig_c
        m_new = jnp.maximum(b_last + m_prev, jnp.max(log_w, axis=0, keepdims=True))
        w_exp = jnp.exp(log_w - m_new)
        decay = jnp.exp(b_last + m_prev - m_new)
        kw_t = (kh.astype(F32) * w_exp).T.astype(BF16)
        c_scr[h] = decay * c_prev + jnp.dot(kw_t, vh, preferred_element_type=F32)
        m_scr[h] = jnp.broadcast_to(m_new, (1, LANE))

        o = jnp.where(lane < M_DK, jax.nn.sigmoid(mo_ref[0, :, sl]) * hval, 0.0)
        ms = jnp.sum(o * o, axis=1, keepdims=True) * (1.0 / M_DK)
        y_ref[0, :, sl] = (o * lax.rsqrt(ms + EPS) * gm_ref[:, sl]).astype(BF16)

    @pl.when(j == pl.num_programs(1) - 1)
    def _():
        cn_ref[0] = c_scr[...]
        mn_ref[0] = m_scr[...]


def _mlstm_call(q, k, v, mo, gates, gates_t, gm, c0, m0, chunk):
    b, t, _ = q.shape
    tok = lambda w: pl.BlockSpec((1, chunk, w), lambda i, j: (i, j, 0))
    st_c = pl.BlockSpec((1, M_HEADS, LANE, LANE), lambda i, j: (i, 0, 0, 0))
    st_m = pl.BlockSpec((1, M_HEADS, 1, LANE), lambda i, j: (i, 0, 0, 0))
    return pl.pallas_call(
        _mlstm_kernel,
        grid=(b, t // chunk),
        in_specs=[tok(MP), tok(MP), tok(MP), tok(MP), tok(LANE),
                  pl.BlockSpec((1, 8, chunk), lambda i, j: (i, 0, j)),
                  pl.BlockSpec((1, MP), lambda i, j: (0, 0)), st_c, st_m],
        out_specs=[tok(MP), st_c, st_m],
        out_shape=[jax.ShapeDtypeStruct((b, t, MP), BF16),
                   jax.ShapeDtypeStruct((b, M_HEADS, LANE, LANE), F32),
                   jax.ShapeDtypeStruct((b, M_HEADS, 1, LANE), F32)],
        scratch_shapes=[pltpu.VMEM((M_HEADS, LANE, LANE), F32), pltpu.VMEM((M_HEADS, 1, LANE), F32)],
        compiler_params=_cparams(("arbitrary", "arbitrary")),
        name="mlstm",
    )(q, k, v, mo, gates, gates_t, gm, c0, m0)


_HALO = 16


def _pool_kernel(u_ref, hist_ref, w_ref, sc_ref, y_ref, ext, *, pos0, carry):
    j = pl.program_id(1)
    tm = u_ref.shape[1]

    @pl.when(j == 0)
    def _():
        ext[0:_HALO, :] = hist_ref[0]

    ext[_HALO:_HALO + tm, :] = u_ref[0]
    cur = ext[_HALO:_HALO + tm, :]
    pos = pos0 + j * tm + lax.broadcasted_iota(jnp.int32, (tm, 1), 0)
    lane = lax.broadcasted_iota(jnp.int32, (tm, POOL_WIDTH), 1)
    run = cur
    d = None
    back = 1
    for gi, w in enumerate(POOL_WINDOWS):
        while back < w:
            run = run + ext[_HALO - back:_HALO - back + tm, :]
            back += 1
        cnt = jnp.minimum(w, pos + 1).astype(F32)
        dg = run / cnt - cur
        d = dg if d is None else jnp.where(lane >= gi * POOL_GDIM, dg, d)
    y = jnp.dot(d.astype(BF16), w_ref[...], preferred_element_type=F32) * sc_ref[...]
    y_ref[0] = y.astype(BF16)
    if carry:
        ext[0:_HALO, :] = ext[tm:tm + _HALO, :]


def _pool_call(u, hist, w_bd, scale, pos0):
    b, t, _ = u.shape
    tm = min(512, t)
    nt = t // tm
    return pl.pallas_call(
        functools.partial(_pool_kernel, pos0=pos0, carry=nt > 1),
        grid=(b, nt),
        in_specs=[pl.BlockSpec((1, tm, POOL_WIDTH), lambda i, j: (i, j, 0)),
                  pl.BlockSpec((1, _HALO, POOL_WIDTH), lambda i, j: (i, 0, 0)),
                  pl.BlockSpec((POOL_WIDTH, POOL_WIDTH), lambda i, j: (0, 0)),
                  pl.BlockSpec((1, POOL_WIDTH), lambda i, j: (0, 0))],
        out_specs=pl.BlockSpec((1, tm, POOL_WIDTH), lambda i, j: (i, j, 0)),
        out_shape=jax.ShapeDtypeStruct((b, t, POOL_WIDTH), BF16),
        scratch_shapes=[pltpu.VMEM((tm + _HALO, POOL_WIDTH), F32)],
        compiler_params=_cparams(("arbitrary", "arbitrary")),
        name="pool",
    )(u, hist, w_bd, scale)


_FF_CHUNK = 1024


def _finish_kernel(x_ref, mm_ref, ma_ref, mp_ref, wo_ref, g_ref, wu_ref, wd_ref, y_ref):
    o_a, o_p = MP, MP + ATTN_WIDTH
    xr = (x_ref[...]
          + jnp.dot(mm_ref[...], wo_ref[0:o_a, :], preferred_element_type=F32)
          + jnp.dot(ma_ref[...], wo_ref[o_a:o_p, :], preferred_element_type=F32)
          + jnp.dot(mp_ref[...], wo_ref[o_p:o_p + POOL_WIDTH, :], preferred_element_type=F32))
    hn = (xr * lax.rsqrt(jnp.mean(xr * xr, axis=-1, keepdims=True) + EPS) * g_ref[...]).astype(BF16)
    mlp = None
    for c in range(D_FF // _FF_CHUNK):
        sl = slice(c * _FF_CHUNK, (c + 1) * _FF_CHUNK)
        hid = jnp.maximum(jnp.dot(hn, wu_ref[:, sl], preferred_element_type=F32), 0.0)
        d = jnp.dot((hid * hid).astype(BF16), wd_ref[sl, :], preferred_element_type=F32)
        mlp = d if mlp is None else mlp + d
    y_ref[...] = xr + mlp


def _finish_call(x, mix_m, mix_a, mix_p, lw):
    t = x.shape[0]
    tm = min(512, t)
    row = lambda w: pl.BlockSpec((tm, w), lambda i: (i, 0))
    return pl.pallas_call(
        _finish_kernel,
        grid=(t // tm,),
        in_specs=[row(D_MODEL), row(MP), row(ATTN_WIDTH), row(POOL_WIDTH),
                  _const_spec((MP + ATTN_WIDTH + POOL_WIDTH, D_MODEL)), _const_spec((1, D_MODEL)),
                  _const_spec((D_MODEL, D_FF)), _const_spec((D_FF, D_MODEL))],
        out_specs=row(D_MODEL),
        out_shape=jax.ShapeDtypeStruct((t, D_MODEL), F32),
        compiler_params=_cparams(("arbitrary",)),
        name="finish",
    )(x, mix_m, mix_a, mix_p, lw["w_out"], lw["g_mlp"], lw["w_up"], lw["w_down"])


_BIG_SLOTS = 1e9


def _row_max(x):
    return jnp.max(x, axis=1, keepdims=True)


def _select_threshold(count_ge, max_below, mn, mx, n_adm, n_inadm, topk):
    kf = float(topk)
    g0 = n_adm + jnp.where(mn <= NEG, n_inadm, 0.0)
    all_kept = g0 < kf
    top = jnp.maximum(mx, NEG)
    hi0 = top + (jnp.abs(top) + 1.0) * 2.0 ** -10
    log_target = float(np.log(kf - 0.5))

    def open_rows(clo):
        return jnp.max(jnp.where(all_kept | (clo == kf), 0.0, 1.0)) > 0.5

    def search_cond(c):
        return (c[0] < MAX_SEARCH) & open_rows(c[3])

    def search(c):
        it, lo, hi, clo, chi = c
        la = jnp.log(jnp.maximum(clo, 1.0))
        lc = jnp.log(jnp.maximum(chi, 0.5))
        frac = jnp.clip((la - log_target) / jnp.maximum(la - lc, 1e-6), 0.05, 0.95)
        frac = jnp.where(it % 2 == 0, frac, 0.5)
        mid = lo + (hi - lo) * frac
        cm = count_ge(mid)
        ge = cm >= kf
        return (it + 1, jnp.where(ge, mid, lo), jnp.where(ge, hi, mid),
                jnp.where(ge, cm, clo), jnp.where(ge, chi, cm))

    _, lo, hi, clo, chi = lax.while_loop(search_cond, search, (jnp.int32(0), mn, hi0, g0, jnp.zeros_like(mn)))
    thr0 = jnp.where(all_kept, mn, lo)
    done0 = jnp.where(all_kept | (clo == kf), 1.0, 0.0)

    def cond(c):
        return (c[0] <= topk) & (jnp.min(c[5]) < 0.5)

    def peel(c):
        it, (hi, chi, thr, slots, done, tie) = c[0], c[1:]
        v = max_below(hi)
        cv = count_ge(v)
        fin = (cv >= kf) & (done < 0.5)
        thr = jnp.where(fin, v, thr)
        slots = jnp.where(fin, kf - chi, slots)
        tie = jnp.where(fin & (cv > kf), 1.0, tie)
        done = jnp.where(fin, 1.0, done)
        live = done < 0.5
        return (it + 1, jnp.where(live, v, hi), jnp.where(live, cv, chi), thr, slots, done, tie)

    init = (jnp.int32(0), hi, chi, thr0, jnp.full_like(mn, _BIG_SLOTS), done0, jnp.zeros_like(mn))
    _, _, _, thr, slots, _, tie = lax.while_loop(cond, peel, init)
    return thr, slots, jnp.max(tie) > 0.5


KEY_GROUP = 2
_SUB = 8


def _dsa_prompt_kernel(iq_ref, aq_ref, g_ref, bound_ref, ik_ref, ak_ref, av_ref, o_ref,
                       s_scr, t_scr, iw_scr, qm_scr, mb_scr, acc_scr, m_scr, l_scr, *, tq, n_total, topk):
    i = pl.program_id(0)
    tk = tq
    nch = tk // LANE
    nrb = tq // _SUB
    grp = KEY_GROUP
    nkb = (i + grp) // grp
    q_pos = i * tq + lax.broadcasted_iota(jnp.int32, (tq, 1), 0)
    gates = g_ref[...]
    for h in range(IDX_HEADS):
        iw_scr[h] = jnp.broadcast_to(gates[:, G_IW + h:G_IW + h + 1], (tq, LANE))

    def tile_scores(kt):
        kk = ik_ref[pl.ds(pl.multiple_of(kt * tk, tk), tk), :]
        s = None
        for h in range(IDX_HEADS):
            sc = lax.dot_general(iq_ref[:, h * LANE:(h + 1) * LANE], kk, _NT, preferred_element_type=F32)
            w = iw_scr[h]
            term = jnp.concatenate([jnp.maximum(sc[:, c * LANE:(c + 1) * LANE], 0.0) * w for c in range(nch)], axis=1)
            s = term if s is None else s + term
        return s

    def fold(mx, mn, hi_side, lo_side):
        for c in range(nch):
            mx = jnp.maximum(mx, hi_side[:, c * LANE:(c + 1) * LANE])
            mn = jnp.minimum(mn, lo_side[:, c * LANE:(c + 1) * LANE])
        return mx, mn

    def full_tile(kt, carry):
        s = tile_scores(kt)
        s_scr[kt] = s
        return fold(*carry, s, s)

    mx, mn = lax.fori_loop(0, i, full_tile,
                           (jnp.full((tq, LANE), -jnp.inf, F32), jnp.full((tq, LANE), jnp.inf, F32)))
    s = tile_scores(i)
    adm = (i * tk + lax.broadcasted_iota(jnp.int32, (1, tk), 1)) <= q_pos
    s_masked = jnp.where(adm, s, -jnp.inf)
    s_scr[i] = s_masked
    mx, mn = fold(mx, mn, s_masked, jnp.where(adm, s, jnp.inf))
    for j in range(1, grp):
        @pl.when(i + j < nkb * grp)
        def _():
            s_scr[i + j] = jnp.full((tq, tk), -jnp.inf, F32)
    mx = _row_max(mx)
    mn = jnp.min(mn, axis=1, keepdims=True)
    n_adm = (q_pos + 1).astype(F32)
    n_inadm = float(n_total) - n_adm

    def scan_scores(t, init, step):
        t_scr[...] = jnp.broadcast_to(t, (tq, LANE))

        def body(kb, accs):
            out = []
            for rb in range(nrb):
                rows = slice(rb * _SUB, (rb + 1) * _SUB)
                tb = t_scr[rows, :]
                a = accs[rb]
                for j in range(grp):
                    for c in range(nch):
                        a = step(a, s_scr[kb * grp + j, rows, c * LANE:(c + 1) * LANE], tb)
                out.append(a)
            return tuple(out)

        accs = lax.fori_loop(0, nkb, body, tuple(jnp.full((_SUB, LANE), init, F32) for _ in range(nrb)))
        return jnp.concatenate(accs, axis=0)

    def count_ge(t):
        acc = scan_scores(t, 0.0, lambda a, sv, tb: a + jnp.where(sv >= tb, 1.0, 0.0))
        return jnp.sum(acc, axis=1, keepdims=True) + jnp.where(t <= NEG, n_inadm, 0.0)

    def max_below(hi):
        acc = scan_scores(hi, -jnp.inf, lambda a, sv, tb: jnp.maximum(a, jnp.where(sv < tb, sv, -jnp.inf)))
        tail = jnp.where((hi > NEG) & (n_inadm > 0.0), NEG, -jnp.inf)
        return jnp.maximum(_row_max(acc), tail)

    thr, slots, any_tie = _select_threshold(count_ge, max_below, mn, mx, n_adm, n_inadm, topk)

    lane = lax.broadcasted_iota(jnp.int32, (tq, LANE), 1)
    for h in range(A_HEADS):
        qp = aq_ref[:, (h // 2) * LANE:(h // 2 + 1) * LANE].astype(F32)
        keep = (lane < HEAD_DIM) if h % 2 == 0 else (lane >= HEAD_DIM)
        qm_scr[h] = jnp.where(keep, qp, 0.0).astype(BF16)
    acc_scr[...] = jnp.zeros_like(acc_scr)
    l_scr[...] = jnp.zeros_like(l_scr)
    m_scr[...] = jnp.full_like(m_scr, MASK_VALUE)
    thr_b = jnp.broadcast_to(thr, (tq, LANE))

    tkb = grp * tk

    def attend(kb, kept):
        rows = pl.ds(pl.multiple_of(kb * tkb, tkb), tkb)
        for p in range(A_HEADS // 2):
            kp = ak_ref[rows, p * LANE:(p + 1) * LANE]
            vp = av_ref[rows, p * LANE:(p + 1) * LANE]
            for h in (2 * p, 2 * p + 1):
                lg = lax.dot_general(qm_scr[h], kp, _NT, preferred_element_type=F32)
                lgm = [jnp.where(kept[c], lg[:, c * LANE:(c + 1) * LANE], MASK_VALUE) for c in range(grp * nch)]
                cm = lgm[0]
                for x in lgm[1:]:
                    cm = jnp.maximum(cm, x)
                m_old = m_scr[h]
                m_new = jnp.maximum(m_old, _row_max(cm))
                alpha = jnp.exp(m_old - m_new)
                pc = [jnp.exp(x - m_new) for x in lgm]
                l_scr[h] = alpha * l_scr[h] + sum(pc)
                pmat = jnp.concatenate(pc, axis=1).astype(BF16)
                acc_scr[h] = alpha * acc_scr[h] + jnp.dot(pmat, vp, preferred_element_type=F32)
                m_scr[h] = m_new

    bound = bound_ref[...]
    neg_b = jnp.broadcast_to(-bound, (tq, LANE))

    def bounded_path():
        def body(kb, _):
            for j in range(grp):
                sv = s_scr[kb * grp + j]
                for c in range(nch):
                    cc = j * nch + c
                    mb_scr[:, cc * LANE:(cc + 1) * LANE] = jnp.where(sv[:, c * LANE:(c + 1) * LANE] >= thr_b,
                                                                     neg_b, -jnp.inf)
            rows = pl.ds(pl.multiple_of(kb * tkb, tkb), tkb)
            for p in range(A_HEADS // 2):
                kp = ak_ref[rows, p * LANE:(p + 1) * LANE]
                vp = av_ref[rows, p * LANE:(p + 1) * LANE]
                for h in (2 * p, 2 * p + 1):
                    lg = lax.dot_general(qm_scr[h], kp, _NT, preferred_element_type=F32)
                    pc = [jnp.exp(lg[:, c * LANE:(c + 1) * LANE] + mb_scr[:, c * LANE:(c + 1) * LANE])
                          for c in range(grp * nch)]
                    l_scr[h] = l_scr[h] + sum(pc)
                    pmat = jnp.concatenate(pc, axis=1).astype(BF16)
                    acc_scr[h] = acc_scr[h] + jnp.dot(pmat, vp, preferred_element_type=F32)
            return 0
        lax.fori_loop(0, nkb, body, 0)

    def general_path():
        r_i = lax.broadcasted_iota(jnp.int32, (tk, tk), 0)
        c_i = lax.broadcasted_iota(jnp.int32, (tk, tk), 1)
        before = jnp.where(r_i < c_i, 1.0, 0.0).astype(BF16)
        slots_b = jnp.broadcast_to(slots, (tq, tk))
        thr_t = jnp.broadcast_to(thr, (tq, tk))

        def body(kb, seen):
            kept = []
            for j in range(grp):
                sv = s_scr[kb * grp + j]
                eq = jnp.where(sv == thr_t, 1.0, 0.0)
                prior = jnp.dot(eq.astype(BF16), before, preferred_element_type=F32) + seen
                kept_f = jnp.where(sv > thr_t, 1.0, jnp.where(prior < slots_b, eq, 0.0))
                kept += [kept_f[:, c * LANE:(c + 1) * LANE] > 0.5 for c in range(nch)]
                seen = seen + jnp.sum(eq, axis=1, keepdims=True)
            attend(kb, kept)
            return seen
        lax.fori_loop(0, nkb, body, jnp.zeros((tq, 1), F32))

    use_bound = jnp.logical_and(jnp.max(bound) <= MAX_SHIFT_BOUND, jnp.logical_not(any_tie))
    lax.cond(use_bound, bounded_path, general_path)

    for p in range(A_HEADS // 2):
        o0 = acc_scr[2 * p] / jnp.sum(l_scr[2 * p], axis=1, keepdims=True)
        o1 = acc_scr[2 * p + 1] / jnp.sum(l_scr[2 * p + 1], axis=1, keepdims=True)
        o_ref[:, p * LANE:(p + 1) * LANE] = jnp.where(lane < HEAD_DIM, o0, o1).astype(BF16)


def _dsa_prompt_call(iq, aq, gates, bound, ikb, akb, avb):
    t = iq.shape[0]
    tq = min(256, t)
    assert (t // tq) % KEY_GROUP == 0
    topk = min(TOPK_MAX, t // 4)
    row = lambda w: pl.BlockSpec((tq, w), lambda i: (i, 0))
    return pl.pallas_call(
        functools.partial(_dsa_prompt_kernel, tq=tq, n_total=t, topk=topk),
        grid=(t // tq,),
        in_specs=[row(IDX_HEADS * LANE), row(ATTN_WIDTH), row(LANE), _const_spec((1, LANE)),
                  _const_spec((t, LANE)), _const_spec((t, ATTN_WIDTH)), _const_spec((t, ATTN_WIDTH))],
        out_specs=row(ATTN_WIDTH),
        out_shape=jax.ShapeDtypeStruct((t, ATTN_WIDTH), BF16),
        scratch_shapes=[pltpu.VMEM((t // tq, tq, tq), F32),
                        pltpu.VMEM((tq, LANE), F32),
                        pltpu.VMEM((IDX_HEADS, tq, LANE), F32),
                        pltpu.VMEM((A_HEADS, tq, LANE), BF16),
                        pltpu.VMEM((tq, KEY_GROUP * tq), F32),
                        pltpu.VMEM((A_HEADS, tq, LANE), F32),
                        pltpu.VMEM((A_HEADS, tq, LANE), F32),
                        pltpu.VMEM((A_HEADS, tq, LANE), F32)],
        compiler_params=_cparams(("arbitrary",)),
        name="dsa_prompt",
    )(iq, aq, gates, bound, ikb, akb, avb)


ROWS_T = 8
PAGES_PER_DMA = 16


def _dsa_sample_kernel(pt_ref, iq_ref, iw_ref, aq_ref, ikn_ref, kn_ref, vn_ref, ci_ref, ck_ref, cv_ref, o_ref,
                       s_scr, ibuf, kbuf, vbuf, isem, ksem, vsem, acc_scr, m_scr, l_scr,
                       *, layer, n_pages, n_new, past_len, topk):
    b = pl.program_id(0)
    page = LANE
    pg = PAGES_PER_DMA
    n_chunks = n_pages // pg
    a_rows = A_HEADS * ROWS_T

    def idx_copies(c, slot):
        return [pltpu.make_async_copy(ci_ref.at[layer, pt_ref[b, c * pg + p]], ibuf.at[slot, p], isem.at[slot])
                for p in range(pg)]

    def kv_copies(c, slot):
        cps = []
        for p in range(pg):
            phys = pt_ref[b, c * pg + p]
            cps.append(pltpu.make_async_copy(ck_ref.at[layer, phys], kbuf.at[slot, p], ksem.at[slot]))
            cps.append(pltpu.make_async_copy(cv_ref.at[layer, phys], vbuf.at[slot, p], vsem.at[slot]))
        return cps

    tok = lax.broadcasted_iota(jnp.int32, (ROWS_T, 1), 0)
    slot_i = lax.broadcasted_iota(jnp.int32, (1, page), 1)
    new_adm = (slot_i <= tok) & (slot_i < n_new)
    iq = iq_ref[0]
    iw = iw_ref[0]

    def page_scores(page_f32):
        sc = jnp.dot(iq, page_f32.astype(BF16), preferred_element_type=F32)
        r = jnp.maximum(sc, 0.0) * iw
        s = r[0:ROWS_T]
        for h in range(1, IDX_HEADS):
            s = s + r[h * ROWS_T:(h + 1) * ROWS_T]
        return s

    for cp in idx_copies(0, 0):
        cp.start()

    def score_chunk(c, carry):
        mx, mn = carry
        slot = c % 2

        @pl.when(c + 1 < n_chunks)
        def _():
            for cp in idx_copies(c + 1, 1 - slot):
                cp.start()

        for cp in idx_copies(c, slot):
            cp.wait()
        for p in range(pg):
            s = page_scores(ibuf[slot, p])
            s_scr[c * pg + p] = s
            mx = jnp.maximum(mx, s)
            mn = jnp.minimum(mn, s)
        return mx, mn

    mx, mn = lax.fori_loop(0, n_chunks, score_chunk,
                           (jnp.full((ROWS_T, page), -jnp.inf, F32), jnp.full((ROWS_T, page), jnp.inf, F32)))
    s_new = page_scores(ikn_ref[0])
    s_scr[n_pages] = jnp.where(new_adm, s_new, -jnp.inf)
    mx = _row_max(jnp.maximum(mx, jnp.where(new_adm, s_new, -jnp.inf)))
    mn = jnp.min(jnp.minimum(mn, jnp.where(new_adm, s_new, jnp.inf)), axis=1, keepdims=True)

    real = tok < n_new
    n_adm = jnp.where(real, (past_len + tok + 1).astype(F32), 0.0)
    n_inadm = jnp.where(real, (n_new - 1 - tok).astype(F32), 0.0)

    for cp in kv_copies(0, 0):
        cp.start()

    def scan_pages(init, step):
        lanes = [jnp.full((ROWS_T, page), init, F32) for _ in range(4)]
        for j in range(n_pages + 1):
            lanes[j % 4] = step(lanes[j % 4], s_scr[j])
        return lanes

    def count_ge(t):
        tb = jnp.broadcast_to(t, (ROWS_T, page))
        parts = scan_pages(0.0, lambda a, sv: a + jnp.where(sv >= tb, 1.0, 0.0))
        acc = (parts[0] + parts[1]) + (parts[2] + parts[3])
        return jnp.sum(acc, axis=1, keepdims=True) + jnp.where(t <= NEG, n_inadm, 0.0)

    def max_below(hi):
        hb = jnp.broadcast_to(hi, (ROWS_T, page))
        parts = scan_pages(-jnp.inf, lambda a, sv: jnp.maximum(a, jnp.where(sv < hb, sv, -jnp.inf)))
        acc = jnp.maximum(jnp.maximum(parts[0], parts[1]), jnp.maximum(parts[2], parts[3]))
        tail = jnp.where((hi > NEG) & (n_inadm > 0.0), NEG, -jnp.inf)
        return jnp.maximum(_row_max(acc), tail)

    thr, slots, any_tie = _select_threshold(count_ge, max_below, mn, mx, n_adm, n_inadm, topk)

    acc_scr[...] = jnp.zeros_like(acc_scr)
    l_scr[...] = jnp.zeros_like(l_scr)
    m_scr[...] = jnp.full_like(m_scr, MASK_VALUE)
    qbd = aq_ref[0]
    thr_b = jnp.broadcast_to(thr, (ROWS_T, page))
    slots_b = jnp.broadcast_to(slots, (ROWS_T, page))
    r_i = lax.broadcasted_iota(jnp.int32, (page, page), 0)
    c_i = lax.broadcasted_iota(jnp.int32, (page, page), 1)
    before = jnp.where(r_i < c_i, 1.0, 0.0).astype(BF16)

    def kept_masks(pages, seen):
        svs = [s_scr[j] for j in pages]

        def plain():
            return [jnp.where(sv >= thr_b, 1.0, 0.0) for sv in svs], seen

        def tie_aware():
            eqs = [jnp.where(sv == thr_b, 1.0, 0.0) for sv in svs]
            local = [jnp.dot(eq.astype(BF16), before, preferred_element_type=F32) for eq in eqs]
            counts = [jnp.sum(eq, axis=1, keepdims=True) for eq in eqs]
            run, out = seen, []
            for sv, eq, loc, cnt in zip(svs, eqs, local, counts):
                out.append(jnp.where(sv > thr_b, 1.0, jnp.where(loc + run < slots_b, eq, 0.0)))
                run = run + cnt
            return out, run

        return lax.cond(any_tie, tie_aware, plain)

    def attend(k_pages, v_pages, kept):
        lgm = []
        for kt, km in zip(k_pages, kept):
            lg = jnp.dot(qbd, kt.astype(BF16), preferred_element_type=F32)
            km_all = jnp.concatenate([km] * A_HEADS, axis=0)
            lgm.append(jnp.where(km_all > 0.5, lg, MASK_VALUE))
        cm = lgm[0]
        for x in lgm[1:]:
            cm = jnp.maximum(cm, x)
        m_old = m_scr[...]
        m_new = jnp.maximum(m_old, _row_max(cm))
        alpha = jnp.exp(m_old - m_new)
        pc = [jnp.exp(x - m_new) for x in lgm]
        l_scr[...] = alpha * l_scr[...] + sum(pc)
        pv = None
        for p_t, vt in zip(pc, v_pages):
            d = lax.dot_general(p_t.astype(BF16), vt.astype(BF16), _NT, preferred_element_type=F32)
            pv = d if pv is None else pv + d
        acc_scr[...] = alpha[:, :1] * acc_scr[...] + pv
        m_scr[...] = m_new

    def chunk(c, seen):
        slot = c % 2

        @pl.when(c + 1 < n_chunks)
        def _():
            for cp in kv_copies(c + 1, 1 - slot):
                cp.start()

        for cp in kv_copies(c, slot):
            cp.wait()
        kept, seen = kept_masks([c * pg + p for p in range(pg)], seen)
        attend([kbuf[slot, p] for p in range(pg)], [vbuf[slot, p] for p in range(pg)], kept)
        return seen

    seen = lax.fori_loop(0, n_chunks, chunk, jnp.zeros((ROWS_T, 1), F32))
    kept, _ = kept_masks([n_pages], seen)
    attend([kn_ref[0]], [vn_ref[0]], kept)

    out_rows = acc_scr[...] / jnp.sum(l_scr[...], axis=1, keepdims=True)
    lane = lax.broadcasted_iota(jnp.int32, (ROWS_T, ATTN_WIDTH), 1)
    out = jnp.zeros((ROWS_T, ATTN_WIDTH), F32)
    for h in range(A_HEADS):
        out = jnp.where(lane // HEAD_DIM == h, out_rows[h * ROWS_T:(h + 1) * ROWS_T], out)
    o_ref[0] = out.astype(BF16)


def _dsa_sample_call(layer, page_table, iq, iw, aq, ik_new, k_new, v_new, ci, ck, cv, n_new):
    b, n_pages = page_table.shape
    page = ci.shape[-1]
    assert page == LANE and n_pages % PAGES_PER_DMA == 0
    past_len = n_pages * page
    topk = min(TOPK_MAX, (past_len + n_new) // 4)
    a_rows = A_HEADS * ROWS_T
    bspec = lambda r, w: pl.BlockSpec((1, r, w), lambda i, pt: (i, 0, 0))
    hbm = pl.BlockSpec(memory_space=pl.ANY)
    pg = PAGES_PER_DMA
    return pl.pallas_call(
        functools.partial(_dsa_sample_kernel, layer=layer, n_pages=n_pages, n_new=n_new, past_len=past_len, topk=topk),
        grid_spec=pltpu.PrefetchScalarGridSpec(
            num_scalar_prefetch=1,
            grid=(b,),
            in_specs=[bspec(IDX_HEADS * ROWS_T, IDX_DIM), bspec(IDX_HEADS * ROWS_T, LANE), bspec(a_rows, ATTN_WIDTH),
                      bspec(IDX_DIM, page), bspec(ATTN_WIDTH, page), bspec(ATTN_WIDTH, page), hbm, hbm, hbm],
            out_specs=bspec(ROWS_T, ATTN_WIDTH),
            scratch_shapes=[pltpu.VMEM((n_pages + 1, ROWS_T, page), F32),
                            pltpu.VMEM((2, pg, IDX_DIM, page), F32),
                            pltpu.VMEM((2, pg, ATTN_WIDTH, page), F32),
                            pltpu.VMEM((2, pg, ATTN_WIDTH, page), F32),
                            pltpu.SemaphoreType.DMA((2,)), pltpu.SemaphoreType.DMA((2,)), pltpu.SemaphoreType.DMA((2,)),
                            pltpu.VMEM((a_rows, ATTN_WIDTH), F32),
                            pltpu.VMEM((a_rows, page), F32),
                            pltpu.VMEM((a_rows, page), F32)]),
        out_shape=jax.ShapeDtypeStruct((b, ROWS_T, ATTN_WIDTH), BF16),
        compiler_params=_cparams(("arbitrary",)),
        name="dsa_sample",
    )(page_table, iq, iw, aq, ik_new, k_new, v_new, ci, ck, cv)


def _pad_heads(a, n_heads, head_dim):
    r = a.shape[0]
    a = a.reshape(r, n_heads, head_dim)
    return jnp.pad(a, ((0, 0), (0, 0), (0, LANE - head_dim))).reshape(r, n_heads * LANE)


def _pack_layer(l, g_mix, w_in, b_i, b_f, g_q, g_k, g_mhead, w_pool, pool_scale, w_out, g_mlp, w_up, w_down):
    mq, mk, mv, mo, mi, mf, aq, ak, av, iq, ik, iw, pu = jnp.split(w_in[l], SPLIT_POINTS, axis=1)
    gate_w = jnp.pad(jnp.concatenate([mi, mf, iw], axis=1), ((0, 0), (0, LANE - 3 * M_HEADS)))
    w_packed = jnp.concatenate(
        [_pad_heads(mq, M_HEADS, M_DK), _pad_heads(mk, M_HEADS, M_DK), _pad_heads(mv, M_HEADS, M_DK),
         _pad_heads(mo, M_HEADS, M_DK), aq, ak, av, _pad_heads(iq, IDX_HEADS, IDX_DIM),
         jnp.pad(ik, ((0, 0), (0, LANE - IDX_DIM))), pu, gate_w], axis=1).astype(BF16)
    gate_bias = jnp.pad(jnp.concatenate([b_i[l], b_f[l]]), (0, LANE - 2 * M_HEADS)).reshape(1, LANE)
    head_id = np.arange(ATTN_WIDTH) // HEAD_DIM
    bd64 = jnp.asarray(head_id[:, None] == head_id[None, :], BF16)
    grp = np.arange(POOL_WIDTH) // POOL_GDIM
    w_bd = jnp.where(jnp.asarray(grp[:, None] == grp[None, :]),
                     jnp.tile(w_pool[l].reshape(POOL_WIDTH, POOL_GDIM), (1, len(POOL_WINDOWS))), 0.0).astype(BF16)
    wo = w_out[l]
    w_out_p = jnp.concatenate([_pad_heads(wo[:MLSTM_WIDTH].T, M_HEADS, M_DK).T, wo[MLSTM_WIDTH:]], axis=0).astype(BF16)
    logit_bound = BOUND_SLACK * (HEAD_DIM ** 0.5) * jnp.max(jnp.abs(g_q[l])) * jnp.max(jnp.abs(g_k[l]))
    return dict(
        logit_bound=jnp.broadcast_to(logit_bound.astype(F32), (1, LANE)),
        g_mix=g_mix[l].reshape(1, D_MODEL), w_in=w_packed, gate_bias=gate_bias,
        g_q=jnp.tile(g_q[l], A_HEADS).reshape(1, ATTN_WIDTH), g_k=jnp.tile(g_k[l], A_HEADS).reshape(1, ATTN_WIDTH),
        bd64=bd64, g_mhead=_pad_heads(g_mhead[l].reshape(1, MLSTM_WIDTH), M_HEADS, M_DK),
        w_pool=w_bd, pool_scale=pool_scale[l].reshape(1, POOL_WIDTH),
        w_out=w_out_p, g_mlp=g_mlp[l].reshape(1, D_MODEL), w_up=w_up[l].astype(BF16), w_down=w_down[l].astype(BF16))


def _rope_tables(pos):
    t = pos.shape[0]
    half = ROPE_DIMS // 2
    inv = ROPE_THETA ** (-jnp.arange(half, dtype=F32) * 2.0 / ROPE_DIMS)
    ang = pos.astype(F32)[:, None] * inv[None, :]
    cos, sin = jnp.cos(ang), jnp.sin(ang)
    rest = HEAD_DIM - ROPE_DIMS
    blk_c = jnp.concatenate([cos, cos, jnp.ones((t, rest), F32)], axis=1)
    blk_s = jnp.concatenate([-sin, sin, jnp.zeros((t, rest), F32)], axis=1)
    pad_c, pad_s = jnp.ones((t, LANE - HEAD_DIM), F32), jnp.zeros((t, LANE - HEAD_DIM), F32)
    return (jnp.concatenate([blk_c, blk_c], axis=1), jnp.concatenate([blk_s, blk_s], axis=1),
            jnp.concatenate([blk_c, pad_c], axis=1), jnp.concatenate([blk_s, pad_s], axis=1))


def _unpad_state(c_aug, m):
    return c_aug[:, :, :M_DK, :M_DK], c_aug[:, :, :M_DK, N_COL], m[:, :, 0, 0]


def _prompt_layer(x, tabs, lw):
    t = x.shape[0]
    pr = _prep_call(x, tabs, lw)
    chunk = min(256, t)
    add_b = lambda a: a[None]
    y_m, c_n, m_n = _mlstm_call(add_b(pr["mq"]), add_b(pr["mk"]), add_b(pr["mv"]), add_b(pr["mo"]),
                                add_b(pr["gates"]), add_b(pr["gates_t"]), lw["g_mhead"],
                                jnp.zeros((1, M_HEADS, LANE, LANE), F32), jnp.zeros((1, M_HEADS, 1, LANE), F32), chunk)
    y_p = _pool_call(add_b(pr["pu"]), jnp.zeros((1, _HALO, POOL_WIDTH), F32), lw["w_pool"], lw["pool_scale"], 0)
    y_a = _dsa_prompt_call(pr["iq"], pr["aq"], pr["gates"], lw["logit_bound"], pr["ikb"], pr["akb"], pr["avb"])
    x_new = _finish_call(x, y_m[0], y_a, y_p[0], lw)
    return (x_new, pr["ak32"].reshape(1, t, A_HEADS, HEAD_DIM), pr["av32"].reshape(1, t, A_HEADS, HEAD_DIM),
            pr["ik32"].reshape(1, t, IDX_DIM), _unpad_state(c_n, m_n), pr["pu"][None, t - POOL_BUF:])


S_CHUNK = 128


def _sample_layer(layer, x, tabs, lw, st_c, st_n, st_m, st_pool, page_table, ci, ck, cv):
    b, t, _ = x.shape
    x2 = x.reshape(b * t, D_MODEL)
    pr = _prep_call(x2, tabs, lw)
    per = lambda a: a.reshape(b, t, a.shape[-1])
    pad_t = lambda a, n: jnp.pad(a, ((0, 0), (0, n - t), (0, 0)))

    gates = per(pr["gates"])
    pad_row = jnp.where(jnp.arange(LANE) < M_HEADS, NEG, 0.0).astype(F32)
    gates_p = jnp.concatenate([gates, jnp.broadcast_to(pad_row, (b, S_CHUNK - t, LANE))], axis=1)
    gates_t = jnp.swapaxes(gates_p[:, :, :8], 1, 2)
    c0 = jnp.pad(st_c, ((0, 0), (0, 0), (0, LANE - M_DK), (0, LANE - M_DK))).at[:, :, :M_DK, N_COL].set(st_n)
    m0 = jnp.broadcast_to(st_m[:, :, None, None], (b, M_HEADS, 1, LANE))
    y_m, c_n, m_n = _mlstm_call(pad_t(per(pr["mq"]), S_CHUNK), pad_t(per(pr["mk"]), S_CHUNK),
                                pad_t(per(pr["mv"]), S_CHUNK), pad_t(per(pr["mo"]), S_CHUNK),
                                gates_p, gates_t, lw["g_mhead"], c0, m0, S_CHUNK)

    past_len = page_table.shape[1] * ci.shape[-1]
    hist = jnp.concatenate([jnp.zeros((b, _HALO - POOL_BUF, POOL_WIDTH), F32), st_pool], axis=1)
    y_p = _pool_call(pad_t(per(pr["pu"]), ROWS_T), hist, lw["w_pool"], lw["pool_scale"], past_len)

    heads_first = lambda a: jnp.pad(jnp.swapaxes(a, 1, 2), ((0, 0), (0, 0), (0, ROWS_T - t), (0, 0)))
    iq = heads_first(per(pr["iq"]).reshape(b, t, IDX_HEADS, LANE)[..., :IDX_DIM])
    iq = iq.reshape(b, IDX_HEADS * ROWS_T, IDX_DIM)
    iw = heads_first(gates[:, :, G_IW:G_IW + IDX_HEADS, None]).reshape(b, IDX_HEADS * ROWS_T, 1)
    iw = jnp.broadcast_to(iw, (b, IDX_HEADS * ROWS_T, LANE))
    aq = heads_first(per(pr["aq"]).reshape(b, t, A_HEADS, HEAD_DIM))
    qbd = aq[:, :, :, None, :] * jnp.eye(A_HEADS, dtype=BF16)[None, :, None, :, None]
    qbd = qbd.reshape(b, A_HEADS * ROWS_T, ATTN_WIDTH)
    to_page = lambda a: jnp.pad(jnp.swapaxes(per(a), 1, 2), ((0, 0), (0, 0), (0, LANE - t)))
    y_a = _dsa_sample_call(layer, page_table, iq, iw, qbd, to_page(pr["ik32"]), to_page(pr["ak32"]),
                           to_page(pr["av32"]), ci, ck, cv, t)

    x_new = _finish_call(x2, y_m[:, :t].reshape(b * t, MP), y_a[:, :t].reshape(b * t, ATTN_WIDTH),
                         y_p[:, :t].reshape(b * t, POOL_WIDTH), lw)
    pool_buf = jnp.concatenate([st_pool, per(pr["pu"])], axis=1)[:, -POOL_BUF:]
    return (x_new.reshape(b, t, D_MODEL), pr["ak32"].reshape(b, t, A_HEADS, HEAD_DIM),
            pr["av32"].reshape(b, t, A_HEADS, HEAD_DIM), per(pr["ik32"]), _unpad_state(c_n, m_n), pool_buf)


def kernel(x_prompt, x_sample, cache_k, cache_v, cache_idx_k, state_C, state_n, state_m, state_pool, page_table,
           g_mix, w_in, b_i, b_f, g_q, g_k, g_mhead, w_pool, pool_scale, w_out, g_mlp, w_up, w_down):
    depth = w_in.shape[0]
    bp, s, _ = x_prompt.shape
    db, t, _ = x_sample.shape
    assert bp == 1 and t <= ROWS_T
    n_pool, page = cache_k.shape[1], cache_k.shape[2]
    past_len = page_table.shape[1] * page
    tabs_p = _rope_tables(jnp.arange(s, dtype=jnp.int32))
    tabs_s = _rope_tables(jnp.tile(past_len + jnp.arange(t, dtype=jnp.int32), db))
    ck = jnp.transpose(cache_k, (0, 1, 3, 4, 2)).reshape(depth, n_pool, ATTN_WIDTH, page)
    cv = jnp.transpose(cache_v, (0, 1, 3, 4, 2)).reshape(depth, n_pool, ATTN_WIDTH, page)
    ci = jnp.transpose(cache_idx_k, (0, 1, 3, 2))

    xp, xs = x_prompt[0], x_sample
    outs_p, outs_s = [], []
    for l in range(depth):
        lw = _pack_layer(l, g_mix, w_in, b_i, b_f, g_q, g_k, g_mhead, w_pool, pool_scale, w_out, g_mlp, w_up, w_down)
        xp, *rest_p = _prompt_layer(xp, tabs_p, lw)
        outs_p.append(rest_p)
        xs, *rest_s = _sample_layer(l, xs, tabs_s, lw, state_C[l], state_n[l], state_m[l], state_pool[l],
                                    page_table, ci, ck, cv)
        outs_s.append(rest_s)

    def stack(outs):
        k, v, ik, st, pool = zip(*outs)
        c, n, m = zip(*st)
        return [jnp.stack(a, axis=0) for a in (k, v, ik, c, n, m, pool)]

    return (xp[None], xs, *stack(outs_p), *stack(outs_s))
```

```python
import functools

import numpy as np
import jax
import jax.numpy as jnp
from jax import lax
from jax.experimental import pallas as pl
from jax.experimental.pallas import tpu as pltpu

F32 = jnp.float32
BF16 = jnp.bfloat16

D_MODEL = 1024
M_HEADS = 4
M_DK = 96
A_HEADS = 6
HEAD_DIM = 64
ATTN_WIDTH = A_HEADS * HEAD_DIM
IDX_HEADS = 4
IDX_DIM = 64
TOPK_MAX = 256
ROPE_THETA = 500000.0
ROPE_DIMS = HEAD_DIM // 4
POOL_WINDOWS = (2, 4, 8, 16)
POOL_GDIM = 64
POOL_WIDTH = 256
POOL_BUF = 15
D_FF = 4 * D_MODEL
EPS = 1e-6
NEG = -1e30
MLSTM_WIDTH = M_HEADS * M_DK
SPLIT_SIZES = (MLSTM_WIDTH, MLSTM_WIDTH, MLSTM_WIDTH, MLSTM_WIDTH, M_HEADS, M_HEADS,
               ATTN_WIDTH, ATTN_WIDTH, ATTN_WIDTH, IDX_HEADS * IDX_DIM, IDX_DIM, IDX_HEADS, POOL_WIDTH)
SPLIT_POINTS = tuple(int(s) for s in np.cumsum(SPLIT_SIZES)[:-1])

LANE = 128
MP = M_HEADS * LANE
N_COL = M_DK
OFF_MQ, OFF_MK, OFF_MV, OFF_MO = 0, MP, 2 * MP, 3 * MP
OFF_AQ = 4 * MP
OFF_AK = OFF_AQ + ATTN_WIDTH
OFF_AV = OFF_AK + ATTN_WIDTH
OFF_IQ = OFF_AV + ATTN_WIDTH
OFF_IK = OFF_IQ + IDX_HEADS * LANE
OFF_PU = OFF_IK + LANE
OFF_G = OFF_PU + POOL_WIDTH
W_PACKED = OFF_G + LANE
G_IG, G_LF, G_IW = 0, M_HEADS, 2 * M_HEADS

VMEM_LIMIT = 56 * 1024 * 1024
MAX_SEARCH = 22
MASK_VALUE = -1e30
MAX_SHIFT_BOUND = 40.0
BOUND_SLACK = 1.02

_NT = (((1,), (1,)), ((), ()))


def _cparams(sem):
    return pltpu.CompilerParams(dimension_semantics=sem, vmem_limit_bytes=VMEM_LIMIT)


def _const_spec(shape):
    nd = len(shape)
    return pl.BlockSpec(shape, lambda *_: (0,) * nd, pipeline_mode=pl.Buffered(1))


def _split3(a):
    a1 = a.astype(BF16)
    r1 = a - a1.astype(F32)
    a2 = r1.astype(BF16)
    a3 = (r1 - a2.astype(F32)).astype(BF16)
    return a1, a2, a3


def _dot_exact_lhs(a, b01):
    return sum(jnp.dot(t, b01, preferred_element_type=F32) for t in _split3(a))


def _dot_exact_rhs(a01, b):
    return sum(jnp.dot(a01, t, preferred_element_type=F32) for t in _split3(b))


def _rope_group(x, c, s, period):
    lane = lax.broadcasted_iota(jnp.int32, x.shape, 1)
    first_half = (lane % period) < (ROPE_DIMS // 2)
    partner = jnp.where(first_half, pltpu.roll(x, LANE - ROPE_DIMS // 2, 1), pltpu.roll(x, ROPE_DIMS // 2, 1))
    return x * c + partner * s


def _rope(x, c, s, period):
    groups = [_rope_group(x[:, g * LANE:(g + 1) * LANE], c, s, period) for g in range(x.shape[1] // LANE)]
    return groups[0] if len(groups) == 1 else jnp.concatenate(groups, axis=1)


def _prep_kernel(x_ref, g_ref, w_ref, bias_ref, gq_ref, gk_ref, c64_ref, s64_ref, c128_ref, s128_ref, bd_ref,
                 mq_o, mk_o, mv_o, mo_o, aq_o, ak32_o, akb_o, av32_o, avb_o, iq_o, ik32_o, ikb_o, pu_o,
                 gates_o, gates_t_o):
    x = x_ref[...]
    h = (x * lax.rsqrt(jnp.mean(x * x, axis=-1, keepdims=True) + EPS) * g_ref[...]).astype(BF16)

    def seg(off, width):
        return jnp.dot(h, w_ref[:, off:off + width], preferred_element_type=F32)

    tm = x.shape[0]
    lane_mp = lax.broadcasted_iota(jnp.int32, (tm, MP), 1)
    mq_o[...] = seg(OFF_MQ, MP).astype(BF16)
    mk_o[...] = (seg(OFF_MK, MP) * (M_DK ** -0.5)).astype(BF16)
    mv_o[...] = jnp.where(lane_mp % LANE == N_COL, 1.0, seg(OFF_MV, MP)).astype(BF16)
    mo_o[...] = seg(OFF_MO, MP)

    c64, s64, c128, s128 = c64_ref[...], s64_ref[...], c128_ref[...], s128_ref[...]
    bd = bd_ref[...]

    def qk_norm(z, gain):
        ms = _dot_exact_lhs(z * z, bd) * (1.0 / HEAD_DIM)
        return _rope(z * lax.rsqrt(ms + EPS) * gain, c64, s64, HEAD_DIM)

    aq = qk_norm(seg(OFF_AQ, ATTN_WIDTH), gq_ref[...])
    aq_o[...] = (aq * (HEAD_DIM ** -0.5)).astype(BF16)
    ak = qk_norm(seg(OFF_AK, ATTN_WIDTH), gk_ref[...])
    ak32_o[...] = ak
    akb_o[...] = ak.astype(BF16)
    av = seg(OFF_AV, ATTN_WIDTH)
    av32_o[...] = av
    avb_o[...] = av.astype(BF16)
    iq_o[...] = _rope(seg(OFF_IQ, IDX_HEADS * LANE), c128, s128, LANE).astype(BF16)
    ik = _rope(seg(OFF_IK, LANE), c128, s128, LANE)
    ik32_o[...] = ik[:, :IDX_DIM]
    ikb_o[...] = ik.astype(BF16)
    pu_o[...] = seg(OFF_PU, POOL_WIDTH)

    zg = seg(OFF_G, LANE)
    zb = zg + bias_ref[...]
    lane = lax.broadcasted_iota(jnp.int32, zg.shape, 1)
    log_sig = jnp.minimum(zb, 0.0) - jnp.log1p(jnp.exp(-jnp.abs(zb)))
    iw_scale = (IDX_HEADS ** -0.5) * (IDX_DIM ** -0.5)
    gates = jnp.where(lane < G_LF, zb,
                      jnp.where(lane < G_IW, log_sig,
                                jnp.where(lane < G_IW + IDX_HEADS, zg * iw_scale, 0.0)))
    gates_o[...] = gates
    gates_t_o[...] = gates.T[:8, :]


def _prep_call(x, pos_tabs, lw):
    t = x.shape[0]
    tm = min(256, t)
    row = lambda w: pl.BlockSpec((tm, w), lambda i: (i, 0))
    out_shapes = dict(
        mq=(MP, BF16), mk=(MP, BF16), mv=(MP, BF16), mo=(MP, F32),
        aq=(ATTN_WIDTH, BF16), ak32=(ATTN_WIDTH, F32), akb=(ATTN_WIDTH, BF16),
        av32=(ATTN_WIDTH, F32), avb=(ATTN_WIDTH, BF16),
        iq=(IDX_HEADS * LANE, BF16), ik32=(IDX_DIM, F32), ikb=(LANE, BF16),
        pu=(POOL_WIDTH, F32), gates=(LANE, F32))
    names = list(out_shapes)
    out_shape = [jax.ShapeDtypeStruct((t, w), dt) for w, dt in out_shapes.values()]
    out_specs = [row(w) for w, _ in out_shapes.values()]
    out_shape.append(jax.ShapeDtypeStruct((8, t), F32))
    out_specs.append(pl.BlockSpec((8, tm), lambda i: (0, i)))
    outs = pl.pallas_call(
        _prep_kernel,
        grid=(t // tm,),
        in_specs=[row(D_MODEL), _const_spec((1, D_MODEL)), _const_spec((D_MODEL, W_PACKED)),
                  _const_spec((1, LANE)), _const_spec((1, ATTN_WIDTH)), _const_spec((1, ATTN_WIDTH)),
                  row(LANE), row(LANE), row(LANE), row(LANE), _const_spec((ATTN_WIDTH, ATTN_WIDTH))],
        out_specs=out_specs,
        out_shape=out_shape,
        compiler_params=_cparams(("arbitrary",)),
        name="prep",
    )(x, lw["g_mix"], lw["w_in"], lw["gate_bias"], lw["g_q"], lw["g_k"], *pos_tabs, lw["bd64"])
    res = dict(zip(names, outs[:-1]))
    res["gates_t"] = outs[-1]
    return res


def _mlstm_kernel(q_ref, k_ref, v_ref, mo_ref, g_ref, gt_ref, gm_ref, c0_ref, m0_ref,
                  y_ref, cn_ref, mn_ref, c_scr, m_scr):
    j = pl.program_id(1)

    @pl.when(j == 0)
    def _():
        c_scr[...] = c0_ref[0]
        m_scr[...] = m0_ref[0]

    l = q_ref.shape[1]
    g = g_ref[0]
    gt = gt_ref[0]
    r_i = lax.broadcasted_iota(jnp.int32, (l, l), 0)
    c_i = lax.broadcasted_iota(jnp.int32, (l, l), 1)
    causal = c_i <= r_i
    tri_l = jnp.where(causal, 1.0, 0.0).astype(BF16)
    tri_u = jnp.where(r_i <= c_i, 1.0, 0.0).astype(BF16)
    bcum_c = _dot_exact_rhs(tri_l, g)
    bcum_r = _dot_exact_lhs(gt, tri_u)
    lane = lax.broadcasted_iota(jnp.int32, (l, LANE), 1)

    for h in range(M_HEADS):
        sl = slice(h * LANE, (h + 1) * LANE)
        qh, kh, vh = q_ref[0, :, sl], k_ref[0, :, sl], v_ref[0, :, sl]
        bc = bcum_c[:, G_LF + h:G_LF + h + 1]
        br = bcum_r[G_LF + h:G_LF + h + 1, :]
        ig_r = gt[G_IG + h:G_IG + h + 1, :]
        ig_c = g[:, G_IG + h:G_IG + h + 1]
        m_prev = m_scr[h][:, :1]
        c_prev = c_scr[h]

        log_d = jnp.where(causal, bc - br + ig_r, -jnp.inf)
        inter = bc + m_prev
        m_t = jnp.maximum(inter, jnp.max(log_d, axis=1, keepdims=True))
        s = lax.dot_general(qh, kh, _NT, preferred_element_type=F32) * jnp.exp(log_d - m_t)
        inter_w = jnp.exp(inter - m_t)
        num = (jnp.dot(s.astype(BF16), vh, preferred_element_type=F32)
               + inter_w * jnp.dot(qh, c_prev.astype(BF16), preferred_element_type=F32))
        den = num[:, N_COL:N_COL + 1]
        hval = num / jnp.maximum(jnp.abs(den), jnp.exp(-m_t))

        b_last = bc[l - 1:l, :]
        log_w = b_last - bc + ig_c
        m_new = jnp.maximum(b_last + m_prev, jnp.max(log_w, axis=0, keepdims=True))
        w_exp = jnp.exp(log_w - m_new)
        decay = jnp.exp(b_last + m_prev - m_new)
        kw_t = (kh.astype(F32) * w_exp).T.astype(BF16)
        c_scr[h] = decay * c_prev + jnp.dot(kw_t, vh, preferred_element_type=F32)
        m_scr[h] = jnp.broadcast_to(m_new, (1, LANE))

        o = jnp.where(lane < M_DK, jax.nn.sigmoid(mo_ref[0, :, sl]) * hval, 0.0)
        ms = jnp.sum(o * o, axis=1, keepdims=True) * (1.0 / M_DK)
        y_ref[0, :, sl] = (o * lax.rsqrt(ms + EPS) * gm_ref[:, sl]).astype(BF16)

    @pl.when(j == pl.num_programs(1) - 1)
    def _():
        cn_ref[0] = c_scr[...]
        mn_ref[0] = m_scr[...]


def _mlstm_call(q, k, v, mo, gates, gates_t, gm, c0, m0, chunk):
    b, t, _ = q.shape
    tok = lambda w: pl.BlockSpec((1, chunk, w), lambda i, j: (i, j, 0))
    st_c = pl.BlockSpec((1, M_HEADS, LANE, LANE), lambda i, j: (i, 0, 0, 0))
    st_m = pl.BlockSpec((1, M_HEADS, 1, LANE), lambda i, j: (i, 0, 0, 0))
    return pl.pallas_call(
        _mlstm_kernel,
        grid=(b, t // chunk),
        in_specs=[tok(MP), tok(MP), tok(MP), tok(MP), tok(LANE),
                  pl.BlockSpec((1, 8, chunk), lambda i, j: (i, 0, j)),
                  pl.BlockSpec((1, MP), lambda i, j: (0, 0)), st_c, st_m],
        out_specs=[tok(MP), st_c, st_m],
        out_shape=[jax.ShapeDtypeStruct((b, t, MP), BF16),
                   jax.ShapeDtypeStruct((b, M_HEADS, LANE, LANE), F32),
                   jax.ShapeDtypeStruct((b, M_HEADS, 1, LANE), F32)],
        scratch_shapes=[pltpu.VMEM((M_HEADS, LANE, LANE), F32), pltpu.VMEM((M_HEADS, 1, LANE), F32)],
        compiler_params=_cparams(("arbitrary", "arbitrary")),
        name="mlstm",
    )(q, k, v, mo, gates, gates_t, gm, c0, m0)


_HALO = 16


def _pool_kernel(u_ref, hist_ref, w_ref, sc_ref, y_ref, ext, *, pos0, carry):
    j = pl.program_id(1)
    tm = u_ref.shape[1]

    @pl.when(j == 0)
    def _():
        ext[0:_HALO, :] = hist_ref[0]

    ext[_HALO:_HALO + tm, :] = u_ref[0]
    cur = ext[_HALO:_HALO + tm, :]
    pos = pos0 + j * tm + lax.broadcasted_iota(jnp.int32, (tm, 1), 0)
    lane = lax.broadcasted_iota(jnp.int32, (tm, POOL_WIDTH), 1)
    run = cur
    d = None
    back = 1
    for gi, w in enumerate(POOL_WINDOWS):
        while back < w:
            run = run + ext[_HALO - back:_HALO - back + tm, :]
            back += 1
        cnt = jnp.minimum(w, pos + 1).astype(F32)
        dg = run / cnt - cur
        d = dg if d is None else jnp.where(lane >= gi * POOL_GDIM, dg, d)
    y = jnp.dot(d.astype(BF16), w_ref[...], preferred_element_type=F32) * sc_ref[...]
    y_ref[0] = y.astype(BF16)
    if carry:
        ext[0:_HALO, :] = ext[tm:tm + _HALO, :]


def _pool_call(u, hist, w_bd, scale, pos0):
    b, t, _ = u.shape
    tm = min(512, t)
    nt = t // tm
    return pl.pallas_call(
        functools.partial(_pool_kernel, pos0=pos0, carry=nt > 1),
        grid=(b, nt),
        in_specs=[pl.BlockSpec((1, tm, POOL_WIDTH), lambda i, j: (i, j, 0)),
                  pl.BlockSpec((1, _HALO, POOL_WIDTH), lambda i, j: (i, 0, 0)),
                  pl.BlockSpec((POOL_WIDTH, POOL_WIDTH), lambda i, j: (0, 0)),
                  pl.BlockSpec((1, POOL_WIDTH), lambda i, j: (0, 0))],
        out_specs=pl.BlockSpec((1, tm, POOL_WIDTH), lambda i, j: (i, j, 0)),
        out_shape=jax.ShapeDtypeStruct((b, t, POOL_WIDTH), BF16),
        scratch_shapes=[pltpu.VMEM((tm + _HALO, POOL_WIDTH), F32)],
        compiler_params=_cparams(("arbitrary", "arbitrary")),
        name="pool",
    )(u, hist, w_bd, scale)


_FF_CHUNK = 1024


def _finish_kernel(x_ref, mm_ref, ma_ref, mp_ref, wo_ref, g_ref, wu_ref, wd_ref, y_ref):
    o_a, o_p = MP, MP + ATTN_WIDTH
    xr = (x_ref[...]
          + jnp.dot(mm_ref[...], wo_ref[0:o_a, :], preferred_element_type=F32)
          + jnp.dot(ma_ref[...], wo_ref[o_a:o_p, :], preferred_element_type=F32)
          + jnp.dot(mp_ref[...], wo_ref[o_p:o_p + POOL_WIDTH, :], preferred_element_type=F32))
    hn = (xr * lax.rsqrt(jnp.mean(xr * xr, axis=-1, keepdims=True) + EPS) * g_ref[...]).astype(BF16)
    mlp = None
    for c in range(D_FF // _FF_CHUNK):
        sl = slice(c * _FF_CHUNK, (c + 1) * _FF_CHUNK)
        hid = jnp.maximum(jnp.dot(hn, wu_ref[:, sl], preferred_element_type=F32), 0.0)
        d = jnp.dot((hid * hid).astype(BF16), wd_ref[sl, :], preferred_element_type=F32)
        mlp = d if mlp is None else mlp + d
    y_ref[...] = xr + mlp


def _finish_call(x, mix_m, mix_a, mix_p, lw):
    t = x.shape[0]
    tm = min(512, t)
    row = lambda w: pl.BlockSpec((tm, w), lambda i: (i, 0))
    return pl.pallas_call(
        _finish_kernel,
        grid=(t // tm,),
        in_specs=[row(D_MODEL), row(MP), row(ATTN_WIDTH), row(POOL_WIDTH),
                  _const_spec((MP + ATTN_WIDTH + POOL_WIDTH, D_MODEL)), _const_spec((1, D_MODEL)),
                  _const_spec((D_MODEL, D_FF)), _const_spec((D_FF, D_MODEL))],
        out_specs=row(D_MODEL),
        out_shape=jax.ShapeDtypeStruct((t, D_MODEL), F32),
        compiler_params=_cparams(("arbitrary",)),
        name="finish",
    )(x, mix_m, mix_a, mix_p, lw["w_out"], lw["g_mlp"], lw["w_up"], lw["w_down"])


_BIG_SLOTS = 1e9


def _row_max(x):
    return jnp.max(x, axis=1, keepdims=True)


def _select_threshold(count_ge, max_below, mn, mx, n_adm, n_inadm, topk):
    kf = float(topk)
    g0 = n_adm + jnp.where(mn <= NEG, n_inadm, 0.0)
    all_kept = g0 < kf
    top = jnp.maximum(mx, NEG)
    hi0 = top + (jnp.abs(top) + 1.0) * 2.0 ** -10

    def open_rows(clo):
        return jnp.max(jnp.where(all_kept | (clo == kf), 0.0, 1.0)) > 0.5

    def search_cond(c):
        return (c[0] < MAX_SEARCH) & open_rows(c[3])

    def search(c):
        it, lo, hi, clo, chi = c
        mid = lo + (hi - lo) * 0.5
        cm = count_ge(mid)
        ge = cm >= kf
        return (it + 1, jnp.where(ge, mid, lo), jnp.where(ge, hi, mid),
                jnp.where(ge, cm, clo), jnp.where(ge, chi, cm))

    _, lo, hi, clo, chi = lax.while_loop(search_cond, search, (jnp.int32(0), mn, hi0, g0, jnp.zeros_like(mn)))
    thr0 = jnp.where(all_kept, mn, lo)
    done0 = jnp.where(all_kept | (clo == kf), 1.0, 0.0)

    def cond(c):
        return (c[0] <= topk) & (jnp.min(c[5]) < 0.5)

    def peel(c):
        it, (hi, chi, thr, slots, done, tie) = c[0], c[1:]
        v = max_below(hi)
        cv = count_ge(v)
        fin = (cv >= kf) & (done < 0.5)
        thr = jnp.where(fin, v, thr)
        slots = jnp.where(fin, kf - chi, slots)
        tie = jnp.where(fin & (cv > kf), 1.0, tie)
        done = jnp.where(fin, 1.0, done)
        live = done < 0.5
        return (it + 1, jnp.where(live, v, hi), jnp.where(live, cv, chi), thr, slots, done, tie)

    init = (jnp.int32(0), hi, chi, thr0, jnp.full_like(mn, _BIG_SLOTS), done0, jnp.zeros_like(mn))
    _, _, _, thr, slots, _, tie = lax.while_loop(cond, peel, init)
    return thr, slots, jnp.max(tie) > 0.5


KEY_GROUP = 2
_SUB = 8


def _dsa_prompt_kernel(iq_ref, aq_ref, g_ref, bound_ref, ik_ref, ak_ref, av_ref, o_ref,
                       s_scr, t_scr, iw_scr, qm_scr, mb_scr, acc_scr, m_scr, l_scr, *, tq, n_total, topk):
    i = pl.program_id(0)
    tk = tq
    nch = tk // LANE
    nrb = tq // _SUB
    grp = KEY_GROUP
    nkb = (i + grp) // grp
    q_pos = i * tq + lax.broadcasted_iota(jnp.int32, (tq, 1), 0)
    gates = g_ref[...]
    for h in range(IDX_HEADS):
        iw_scr[h] = jnp.broadcast_to(gates[:, G_IW + h:G_IW + h + 1], (tq, LANE))

    def tile_scores(kt):
        kk = ik_ref[pl.ds(pl.multiple_of(kt * tk, tk), tk), :]
        s = None
        for h in range(IDX_HEADS):
            sc = lax.dot_general(iq_ref[:, h * LANE:(h + 1) * LANE], kk, _NT, preferred_element_type=F32)
            w = iw_scr[h]
            term = jnp.concatenate([jnp.maximum(sc[:, c * LANE:(c + 1) * LANE], 0.0) * w for c in range(nch)], axis=1)
            s = term if s is None else s + term
        return s

    def fold(mx, mn, hi_side, lo_side):
        for c in range(nch):
            mx = jnp.maximum(mx, hi_side[:, c * LANE:(c + 1) * LANE])
            mn = jnp.minimum(mn, lo_side[:, c * LANE:(c + 1) * LANE])
        return mx, mn

    def score_block(kb, carry):
        mx, mn = carry
        for j in range(grp):
            kt = kb * grp + j
            s = tile_scores(kt)
            adm = (kt * tk + lax.broadcasted_iota(jnp.int32, (1, tk), 1)) <= q_pos
            s_masked = jnp.where(adm, s, -jnp.inf)
            s_scr[kt] = s_masked
            mx, mn = fold(mx, mn, s_masked, jnp.where(adm, s, jnp.inf))
        return mx, mn

    mx, mn = lax.fori_loop(0, nkb, score_block,
                           (jnp.full((tq, LANE), -jnp.inf, F32), jnp.full((tq, LANE), jnp.inf, F32)))
    mx = _row_max(mx)
    mn = jnp.min(mn, axis=1, keepdims=True)
    n_adm = (q_pos + 1).astype(F32)
    n_inadm = float(n_total) - n_adm

    def scan_scores(t, init, step):
        t_scr[...] = jnp.broadcast_to(t, (tq, LANE))

        def body(kb, accs):
            out = []
            for rb in range(nrb):
                rows = slice(rb * _SUB, (rb + 1) * _SUB)
                tb = t_scr[rows, :]
                a = accs[rb]
                for j in range(grp):
                    for c in range(nch):
                        a = step(a, s_scr[kb * grp + j, rows, c * LANE:(c + 1) * LANE], tb)
                out.append(a)
            return tuple(out)

        accs = lax.fori_loop(0, nkb, body, tuple(jnp.full((_SUB, LANE), init, F32) for _ in range(nrb)))
        return jnp.concatenate(accs, axis=0)

    def count_ge(t):
        acc = scan_scores(t, 0.0, lambda a, sv, tb: a + jnp.where(sv >= tb, 1.0, 0.0))
        return jnp.sum(acc, axis=1, keepdims=True) + jnp.where(t <= NEG, n_inadm, 0.0)

    def max_below(hi):
        acc = scan_scores(hi, -jnp.inf, lambda a, sv, tb: jnp.maximum(a, jnp.where(sv < tb, sv, -jnp.inf)))
        tail = jnp.where((hi > NEG) & (n_inadm > 0.0), NEG, -jnp.inf)
        return jnp.maximum(_row_max(acc), tail)

    thr, slots, any_tie = _select_threshold(count_ge, max_below, mn, mx, n_adm, n_inadm, topk)

    lane = lax.broadcasted_iota(jnp.int32, (tq, LANE), 1)
    for h in range(A_HEADS):
        qp = aq_ref[:, (h // 2) * LANE:(h // 2 + 1) * LANE].astype(F32)
        keep = (lane < HEAD_DIM) if h % 2 == 0 else (lane >= HEAD_DIM)
        qm_scr[h] = jnp.where(keep, qp, 0.0).astype(BF16)
    acc_scr[...] = jnp.zeros_like(acc_scr)
    l_scr[...] = jnp.zeros_like(l_scr)
    m_scr[...] = jnp.full_like(m_scr, MASK_VALUE)
    thr_b = jnp.broadcast_to(thr, (tq, LANE))

    tkb = grp * tk

    def attend(kb, kept):
        rows = pl.ds(pl.multiple_of(kb * tkb, tkb), tkb)
        for p in range(A_HEADS // 2):
            kp = ak_ref[rows, p * LANE:(p + 1) * LANE]
            vp = av_ref[rows, p * LANE:(p + 1) * LANE]
            for h in (2 * p, 2 * p + 1):
                lg = lax.dot_general(qm_scr[h], kp, _NT, preferred_element_type=F32)
                lgm = [jnp.where(kept[c], lg[:, c * LANE:(c + 1) * LANE], MASK_VALUE) for c in range(grp * nch)]
                cm = lgm[0]
                for x in lgm[1:]:
                    cm = jnp.maximum(cm, x)
                m_old = m_scr[h]
                m_new = jnp.maximum(m_old, _row_max(cm))
                alpha = jnp.exp(m_old - m_new)
                pc = [jnp.exp(x - m_new) for x in lgm]
                l_scr[h] = alpha * l_scr[h] + sum(pc)
                pmat = jnp.concatenate(pc, axis=1).astype(BF16)
                acc_scr[h] = alpha * acc_scr[h] + jnp.dot(pmat, vp, preferred_element_type=F32)
                m_scr[h] = m_new

    bound = bound_ref[...]
    neg_b = jnp.broadcast_to(-bound, (tq, LANE))

    def bounded_path():
        def body(kb, _):
            for j in range(grp):
                sv = s_scr[kb * grp + j]
                for c in range(nch):
                    cc = j * nch + c
                    mb_scr[:, cc * LANE:(cc + 1) * LANE] = jnp.where(sv[:, c * LANE:(c + 1) * LANE] >= thr_b,
                                                                     neg_b, -jnp.inf)
            rows = pl.ds(pl.multiple_of(kb * tkb, tkb), tkb)
            for p in range(A_HEADS // 2):
                kp = ak_ref[rows, p * LANE:(p + 1) * LANE]
                vp = av_ref[rows, p * LANE:(p + 1) * LANE]
                for h in (2 * p, 2 * p + 1):
                    lg = lax.dot_general(qm_scr[h], kp, _NT, preferred_element_type=F32)
                    pc = [jnp.exp(lg[:, c * LANE:(c + 1) * LANE] + mb_scr[:, c * LANE:(c + 1) * LANE])
                          for c in range(grp * nch)]
                    l_scr[h] = l_scr[h] + sum(pc)
                    pmat = jnp.concatenate(pc, axis=1).astype(BF16)
                    acc_scr[h] = acc_scr[h] + jnp.dot(pmat, vp, preferred_element_type=F32)
            return 0
        lax.fori_loop(0, nkb, body, 0)

    def general_path():
        r_i = lax.broadcasted_iota(jnp.int32, (tk, tk), 0)
        c_i = lax.broadcasted_iota(jnp.int32, (tk, tk), 1)
        before = jnp.where(r_i < c_i, 1.0, 0.0).astype(BF16)
        slots_b = jnp.broadcast_to(slots, (tq, tk))
        thr_t = jnp.broadcast_to(thr, (tq, tk))

        def body(kb, seen):
            kept = []
            for j in range(grp):
                sv = s_scr[kb * grp + j]
                eq = jnp.where(sv == thr_t, 1.0, 0.0)
                prior = jnp.dot(eq.astype(BF16), before, preferred_element_type=F32) + seen
                kept_f = jnp.where(sv > thr_t, 1.0, jnp.where(prior < slots_b, eq, 0.0))
                kept += [kept_f[:, c * LANE:(c + 1) * LANE] > 0.5 for c in range(nch)]
                seen = seen + jnp.sum(eq, axis=1, keepdims=True)
            attend(kb, kept)
            return seen
        lax.fori_loop(0, nkb, body, jnp.zeros((tq, 1), F32))

    use_bound = jnp.logical_and(jnp.max(bound) <= MAX_SHIFT_BOUND, jnp.logical_not(any_tie))
    lax.cond(use_bound, bounded_path, general_path)

    for p in range(A_HEADS // 2):
        o0 = acc_scr[2 * p] / jnp.sum(l_scr[2 * p], axis=1, keepdims=True)
        o1 = acc_scr[2 * p + 1] / jnp.sum(l_scr[2 * p + 1], axis=1, keepdims=True)
        o_ref[:, p * LANE:(p + 1) * LANE] = jnp.where(lane < HEAD_DIM, o0, o1).astype(BF16)


def _dsa_prompt_call(iq, aq, gates, bound, ikb, akb, avb):
    t = iq.shape[0]
    tq = min(256, t)
    assert (t // tq) % KEY_GROUP == 0
    topk = min(TOPK_MAX, t // 4)
    row = lambda w: pl.BlockSpec((tq, w), lambda i: (i, 0))
    return pl.pallas_call(
        functools.partial(_dsa_prompt_kernel, tq=tq, n_total=t, topk=topk),
        grid=(t // tq,),
        in_specs=[row(IDX_HEADS * LANE), row(ATTN_WIDTH), row(LANE), _const_spec((1, LANE)),
                  _const_spec((t, LANE)), _const_spec((t, ATTN_WIDTH)), _const_spec((t, ATTN_WIDTH))],
        out_specs=row(ATTN_WIDTH),
        out_shape=jax.ShapeDtypeStruct((t, ATTN_WIDTH), BF16),
        scratch_shapes=[pltpu.VMEM((t // tq, tq, tq), F32),
                        pltpu.VMEM((tq, LANE), F32),
                        pltpu.VMEM((IDX_HEADS, tq, LANE), F32),
                        pltpu.VMEM((A_HEADS, tq, LANE), BF16),
                        pltpu.VMEM((tq, KEY_GROUP * tq), F32),
                        pltpu.VMEM((A_HEADS, tq, LANE), F32),
                        pltpu.VMEM((A_HEADS, tq, LANE), F32),
                        pltpu.VMEM((A_HEADS, tq, LANE), F32)],
        compiler_params=_cparams(("arbitrary",)),
        name="dsa_prompt",
    )(iq, aq, gates, bound, ikb, akb, avb)


ROWS_T = 8
PAGES_PER_DMA = 16


def _dsa_sample_kernel(pt_ref, iq_ref, iw_ref, aq_ref, ikn_ref, kn_ref, vn_ref, ci_ref, ck_ref, cv_ref, o_ref,
                       s_scr, ibuf, kbuf, vbuf, isem, ksem, vsem, acc_scr, m_scr, l_scr,
                       *, layer, n_pages, n_new, past_len, topk):
    b = pl.program_id(0)
    page = LANE
    pg = PAGES_PER_DMA
    n_chunks = n_pages // pg
    a_rows = A_HEADS * ROWS_T

    def idx_copies(c, slot):
        return [pltpu.make_async_copy(ci_ref.at[layer, pt_ref[b, c * pg + p]], ibuf.at[slot, p], isem.at[slot])
                for p in range(pg)]

    def kv_copies(c, slot):
        cps = []
        for p in range(pg):
            phys = pt_ref[b, c * pg + p]
            cps.append(pltpu.make_async_copy(ck_ref.at[layer, phys], kbuf.at[slot, p], ksem.at[slot]))
            cps.append(pltpu.make_async_copy(cv_ref.at[layer, phys], vbuf.at[slot, p], vsem.at[slot]))
        return cps

    tok = lax.broadcasted_iota(jnp.int32, (ROWS_T, 1), 0)
    slot_i = lax.broadcasted_iota(jnp.int32, (1, page), 1)
    new_adm = (slot_i <= tok) & (slot_i < n_new)
    iq = iq_ref[0]
    iw = iw_ref[0]

    def page_scores(page_f32):
        sc = jnp.dot(iq, page_f32.astype(BF16), preferred_element_type=F32)
        r = jnp.maximum(sc, 0.0) * iw
        s = r[0:ROWS_T]
        for h in range(1, IDX_HEADS):
            s = s + r[h * ROWS_T:(h + 1) * ROWS_T]
        return s

    for cp in idx_copies(0, 0):
        cp.start()

    def score_chunk(c, carry):
        mx, mn = carry
        slot = c % 2

        @pl.when(c + 1 < n_chunks)
        def _():
            for cp in idx_copies(c + 1, 1 - slot):
                cp.start()

        for cp in idx_copies(c, slot):
            cp.wait()
        for p in range(pg):
            s = page_scores(ibuf[slot, p])
            s_scr[c * pg + p] = s
            mx = jnp.maximum(mx, s)
            mn = jnp.minimum(mn, s)
        return mx, mn

    mx, mn = lax.fori_loop(0, n_chunks, score_chunk,
                           (jnp.full((ROWS_T, page), -jnp.inf, F32), jnp.full((ROWS_T, page), jnp.inf, F32)))
    s_new = page_scores(ikn_ref[0])
    s_scr[n_pages] = jnp.where(new_adm, s_new, -jnp.inf)
    mx = _row_max(jnp.maximum(mx, jnp.where(new_adm, s_new, -jnp.inf)))
    mn = jnp.min(jnp.minimum(mn, jnp.where(new_adm, s_new, jnp.inf)), axis=1, keepdims=True)

    real = tok < n_new
    n_adm = jnp.where(real, (past_len + tok + 1).astype(F32), 0.0)
    n_inadm = jnp.where(real, (n_new - 1 - tok).astype(F32), 0.0)

    for cp in kv_copies(0, 0):
        cp.start()

    def scan_pages(init, step):
        lanes = [jnp.full((ROWS_T, page), init, F32) for _ in range(4)]
        for j in range(n_pages + 1):
            lanes[j % 4] = step(lanes[j % 4], s_scr[j])
        return lanes

    def count_ge(t):
        tb = jnp.broadcast_to(t, (ROWS_T, page))
        parts = scan_pages(0.0, lambda a, sv: a + jnp.where(sv >= tb, 1.0, 0.0))
        acc = (parts[0] + parts[1]) + (parts[2] + parts[3])
        return jnp.sum(acc, axis=1, keepdims=True) + jnp.where(t <= NEG, n_inadm, 0.0)

    def max_below(hi):
        hb = jnp.broadcast_to(hi, (ROWS_T, page))
        parts = scan_pages(-jnp.inf, lambda a, sv: jnp.maximum(a, jnp.where(sv < hb, sv, -jnp.inf)))
        acc = jnp.maximum(jnp.maximum(parts[0], parts[1]), jnp.maximum(parts[2], parts[3]))
        tail = jnp.where((hi > NEG) & (n_inadm > 0.0), NEG, -jnp.inf)
        return jnp.maximum(_row_max(acc), tail)

    thr, slots, any_tie = _select_threshold(count_ge, max_below, mn, mx, n_adm, n_inadm, topk)

    acc_scr[...] = jnp.zeros_like(acc_scr)
    l_scr[...] = jnp.zeros_like(l_scr)
    m_scr[...] = jnp.full_like(m_scr, MASK_VALUE)
    qbd = aq_ref[0]
    thr_b = jnp.broadcast_to(thr, (ROWS_T, page))
    slots_b = jnp.broadcast_to(slots, (ROWS_T, page))
    r_i = lax.broadcasted_iota(jnp.int32, (page, page), 0)
    c_i = lax.broadcasted_iota(jnp.int32, (page, page), 1)
    before = jnp.where(r_i < c_i, 1.0, 0.0).astype(BF16)

    def kept_masks(pages, seen):
        svs = [s_scr[j] for j in pages]

        def plain():
            return [jnp.where(sv >= thr_b, 1.0, 0.0) for sv in svs], seen

        def tie_aware():
            eqs = [jnp.where(sv == thr_b, 1.0, 0.0) for sv in svs]
            local = [jnp.dot(eq.astype(BF16), before, preferred_element_type=F32) for eq in eqs]
            counts = [jnp.sum(eq, axis=1, keepdims=True) for eq in eqs]
            run, out = seen, []
            for sv, eq, loc, cnt in zip(svs, eqs, local, counts):
                out.append(jnp.where(sv > thr_b, 1.0, jnp.where(loc + run < slots_b, eq, 0.0)))
                run = run + cnt
            return out, run

        return lax.cond(any_tie, tie_aware, plain)

    def attend(k_pages, v_pages, kept):
        lgm = []
        for kt, km in zip(k_pages, kept):
            lg = jnp.dot(qbd, kt.astype(BF16), preferred_element_type=F32)
            km_all = jnp.concatenate([km] * A_HEADS, axis=0)
            lgm.append(jnp.where(km_all > 0.5, lg, MASK_VALUE))
        cm = lgm[0]
        for x in lgm[1:]:
            cm = jnp.maximum(cm, x)
        m_old = m_scr[...]
        m_new = jnp.maximum(m_old, _row_max(cm))
        alpha = jnp.exp(m_old - m_new)
        pc = [jnp.exp(x - m_new) for x in lgm]
        l_scr[...] = alpha * l_scr[...] + sum(pc)
        pv = None
        for p_t, vt in zip(pc, v_pages):
            d = lax.dot_general(p_t.astype(BF16), vt.astype(BF16), _NT, preferred_element_type=F32)
            pv = d if pv is None else pv + d
        acc_scr[...] = alpha[:, :1] * acc_scr[...] + pv
        m_scr[...] = m_new

    def chunk(c, seen):
        slot = c % 2

        @pl.when(c + 1 < n_chunks)
        def _():
            for cp in kv_copies(c + 1, 1 - slot):
                cp.start()

        for cp in kv_copies(c, slot):
            cp.wait()
        kept, seen = kept_masks([c * pg + p for p in range(pg)], seen)
        attend([kbuf[slot, p] for p in range(pg)], [vbuf[slot, p] for p in range(pg)], kept)
        return seen

    seen = lax.fori_loop(0, n_chunks, chunk, jnp.zeros((ROWS_T, 1), F32))
    kept, _ = kept_masks([n_pages], seen)
    attend([kn_ref[0]], [vn_ref[0]], kept)

    out_rows = acc_scr[...] / jnp.sum(l_scr[...], axis=1, keepdims=True)
    lane = lax.broadcasted_iota(jnp.int32, (ROWS_T, ATTN_WIDTH), 1)
    out = jnp.zeros((ROWS_T, ATTN_WIDTH), F32)
    for h in range(A_HEADS):
        out = jnp.where(lane // HEAD_DIM == h, out_rows[h * ROWS_T:(h + 1) * ROWS_T], out)
    o_ref[0] = out.astype(BF16)


def _dsa_sample_call(layer, page_table, iq, iw, aq, ik_new, k_new, v_new, ci, ck, cv, n_new):
    b, n_pages = page_table.shape
    page = ci.shape[-1]
    assert page == LANE and n_pages % PAGES_PER_DMA == 0
    past_len = n_pages * page
    topk = min(TOPK_MAX, (past_len + n_new) // 4)
    a_rows = A_HEADS * ROWS_T
    bspec = lambda r, w: pl.BlockSpec((1, r, w), lambda i, pt: (i, 0, 0))
    hbm = pl.BlockSpec(memory_space=pl.ANY)
    pg = PAGES_PER_DMA
    return pl.pallas_call(
        functools.partial(_dsa_sample_kernel, layer=layer, n_pages=n_pages, n_new=n_new, past_len=past_len, topk=topk),
        grid_spec=pltpu.PrefetchScalarGridSpec(
            num_scalar_prefetch=1,
            grid=(b,),
            in_specs=[bspec(IDX_HEADS * ROWS_T, IDX_DIM), bspec(IDX_HEADS * ROWS_T, LANE), bspec(a_rows, ATTN_WIDTH),
                      bspec(IDX_DIM, page), bspec(ATTN_WIDTH, page), bspec(ATTN_WIDTH, page), hbm, hbm, hbm],
            out_specs=bspec(ROWS_T, ATTN_WIDTH),
            scratch_shapes=[pltpu.VMEM((n_pages + 1, ROWS_T, page), F32),
                            pltpu.VMEM((2, pg, IDX_DIM, page), F32),
                            pltpu.VMEM((2, pg, ATTN_WIDTH, page), F32),
                            pltpu.VMEM((2, pg, ATTN_WIDTH, page), F32),
                            pltpu.SemaphoreType.DMA((2,)), pltpu.SemaphoreType.DMA((2,)), pltpu.SemaphoreType.DMA((2,)),
                            pltpu.VMEM((a_rows, ATTN_WIDTH), F32),
                            pltpu.VMEM((a_rows, page), F32),
                            pltpu.VMEM((a_rows, page), F32)]),
        out_shape=jax.ShapeDtypeStruct((b, ROWS_T, ATTN_WIDTH), BF16),
        compiler_params=_cparams(("arbitrary",)),
        name="dsa_sample",
    )(page_table, iq, iw, aq, ik_new, k_new, v_new, ci, ck, cv)


def _pad_heads(a, n_heads, head_dim):
    r = a.shape[0]
    a = a.reshape(r, n_heads, head_dim)
    return jnp.pad(a, ((0, 0), (0, 0), (0, LANE - head_dim))).reshape(r, n_heads * LANE)


def _pack_layer(l, g_mix, w_in, b_i, b_f, g_q, g_k, g_mhead, w_pool, pool_scale, w_out, g_mlp, w_up, w_down):
    mq, mk, mv, mo, mi, mf, aq, ak, av, iq, ik, iw, pu = jnp.split(w_in[l], SPLIT_POINTS, axis=1)
    gate_w = jnp.pad(jnp.concatenate([mi, mf, iw], axis=1), ((0, 0), (0, LANE - 3 * M_HEADS)))
    w_packed = jnp.concatenate(
        [_pad_heads(mq, M_HEADS, M_DK), _pad_heads(mk, M_HEADS, M_DK), _pad_heads(mv, M_HEADS, M_DK),
         _pad_heads(mo, M_HEADS, M_DK), aq, ak, av, _pad_heads(iq, IDX_HEADS, IDX_DIM),
         jnp.pad(ik, ((0, 0), (0, LANE - IDX_DIM))), pu, gate_w], axis=1).astype(BF16)
    gate_bias = jnp.pad(jnp.concatenate([b_i[l], b_f[l]]), (0, LANE - 2 * M_HEADS)).reshape(1, LANE)
    head_id = np.arange(ATTN_WIDTH) // HEAD_DIM
    bd64 = jnp.asarray(head_id[:, None] == head_id[None, :], BF16)
    grp = np.arange(POOL_WIDTH) // POOL_GDIM
    w_bd = jnp.where(jnp.asarray(grp[:, None] == grp[None, :]),
                     jnp.tile(w_pool[l].reshape(POOL_WIDTH, POOL_GDIM), (1, len(POOL_WINDOWS))), 0.0).astype(BF16)
    wo = w_out[l]
    w_out_p = jnp.concatenate([_pad_heads(wo[:MLSTM_WIDTH].T, M_HEADS, M_DK).T, wo[MLSTM_WIDTH:]], axis=0).astype(BF16)
    logit_bound = BOUND_SLACK * (HEAD_DIM ** 0.5) * jnp.max(jnp.abs(g_q[l])) * jnp.max(jnp.abs(g_k[l]))
    return dict(
        logit_bound=jnp.broadcast_to(logit_bound.astype(F32), (1, LANE)),
        g_mix=g_mix[l].reshape(1, D_MODEL), w_in=w_packed, gate_bias=gate_bias,
        g_q=jnp.tile(g_q[l], A_HEADS).reshape(1, ATTN_WIDTH), g_k=jnp.tile(g_k[l], A_HEADS).reshape(1, ATTN_WIDTH),
        bd64=bd64, g_mhead=_pad_heads(g_mhead[l].reshape(1, MLSTM_WIDTH), M_HEADS, M_DK),
        w_pool=w_bd, pool_scale=pool_scale[l].reshape(1, POOL_WIDTH),
        w_out=w_out_p, g_mlp=g_mlp[l].reshape(1, D_MODEL), w_up=w_up[l].astype(BF16), w_down=w_down[l].astype(BF16))


def _rope_tables(pos):
    t = pos.shape[0]
    half = ROPE_DIMS // 2
    inv = ROPE_THETA ** (-jnp.arange(half, dtype=F32) * 2.0 / ROPE_DIMS)
    ang = pos.astype(F32)[:, None] * inv[None, :]
    cos, sin = jnp.cos(ang), jnp.sin(ang)
    rest = HEAD_DIM - ROPE_DIMS
    blk_c = jnp.concatenate([cos, cos, jnp.ones((t, rest), F32)], axis=1)
    blk_s = jnp.concatenate([-sin, sin, jnp.zeros((t, rest), F32)], axis=1)
    pad_c, pad_s = jnp.ones((t, LANE - HEAD_DIM), F32), jnp.zeros((t, LANE - HEAD_DIM), F32)
    return (jnp.concatenate([blk_c, blk_c], axis=1), jnp.concatenate([blk_s, blk_s], axis=1),
            jnp.concatenate([blk_c, pad_c], axis=1), jnp.concatenate([blk_s, pad_s], axis=1))


def _unpad_state(c_aug, m):
    return c_aug[:, :, :M_DK, :M_DK], c_aug[:, :, :M_DK, N_COL], m[:, :, 0, 0]


def _prompt_layer(x, tabs, lw):
    t = x.shape[0]
    pr = _prep_call(x, tabs, lw)
    chunk = min(256, t)
    add_b = lambda a: a[None]
    y_m, c_n, m_n = _mlstm_call(add_b(pr["mq"]), add_b(pr["mk"]), add_b(pr["mv"]), add_b(pr["mo"]),
                                add_b(pr["gates"]), add_b(pr["gates_t"]), lw["g_mhead"],
                                jnp.zeros((1, M_HEADS, LANE, LANE), F32), jnp.zeros((1, M_HEADS, 1, LANE), F32), chunk)
    y_p = _pool_call(add_b(pr["pu"]), jnp.zeros((1, _HALO, POOL_WIDTH), F32), lw["w_pool"], lw["pool_scale"], 0)
    y_a = _dsa_prompt_call(pr["iq"], pr["aq"], pr["gates"], lw["logit_bound"], pr["ikb"], pr["akb"], pr["avb"])
    x_new = _finish_call(x, y_m[0], y_a, y_p[0], lw)
    return (x_new, pr["ak32"].reshape(1, t, A_HEADS, HEAD_DIM), pr["av32"].reshape(1, t, A_HEADS, HEAD_DIM),
            pr["ik32"].reshape(1, t, IDX_DIM), _unpad_state(c_n, m_n), pr["pu"][None, t - POOL_BUF:])


S_CHUNK = 128


def _sample_layer(layer, x, tabs, lw, st_c, st_n, st_m, st_pool, page_table, ci, ck, cv):
    b, t, _ = x.shape
    x2 = x.reshape(b * t, D_MODEL)
    pr = _prep_call(x2, tabs, lw)
    per = lambda a: a.reshape(b, t, a.shape[-1])
    pad_t = lambda a, n: jnp.pad(a, ((0, 0), (0, n - t), (0, 0)))

    gates = per(pr["gates"])
    pad_row = jnp.where(jnp.arange(LANE) < M_HEADS, NEG, 0.0).astype(F32)
    gates_p = jnp.concatenate([gates, jnp.broadcast_to(pad_row, (b, S_CHUNK - t, LANE))], axis=1)
    gates_t = jnp.swapaxes(gates_p[:, :, :8], 1, 2)
    c0 = jnp.pad(st_c, ((0, 0), (0, 0), (0, LANE - M_DK), (0, LANE - M_DK))).at[:, :, :M_DK, N_COL].set(st_n)
    m0 = jnp.broadcast_to(st_m[:, :, None, None], (b, M_HEADS, 1, LANE))
    y_m, c_n, m_n = _mlstm_call(pad_t(per(pr["mq"]), S_CHUNK), pad_t(per(pr["mk"]), S_CHUNK),
                                pad_t(per(pr["mv"]), S_CHUNK), pad_t(per(pr["mo"]), S_CHUNK),
                                gates_p, gates_t, lw["g_mhead"], c0, m0, S_CHUNK)

    past_len = page_table.shape[1] * ci.shape[-1]
    hist = jnp.concatenate([jnp.zeros((b, _HALO - POOL_BUF, POOL_WIDTH), F32), st_pool], axis=1)
    y_p = _pool_call(pad_t(per(pr["pu"]), ROWS_T), hist, lw["w_pool"], lw["pool_scale"], past_len)

    heads_first = lambda a: jnp.pad(jnp.swapaxes(a, 1, 2), ((0, 0), (0, 0), (0, ROWS_T - t), (0, 0)))
    iq = heads_first(per(pr["iq"]).reshape(b, t, IDX_HEADS, LANE)[..., :IDX_DIM])
    iq = iq.reshape(b, IDX_HEADS * ROWS_T, IDX_DIM)
    iw = heads_first(gates[:, :, G_IW:G_IW + IDX_HEADS, None]).reshape(b, IDX_HEADS * ROWS_T, 1)
    iw = jnp.broadcast_to(iw, (b, IDX_HEADS * ROWS_T, LANE))
    aq = heads_first(per(pr["aq"]).reshape(b, t, A_HEADS, HEAD_DIM))
    qbd = aq[:, :, :, None, :] * jnp.eye(A_HEADS, dtype=BF16)[None, :, None, :, None]
    qbd = qbd.reshape(b, A_HEADS * ROWS_T, ATTN_WIDTH)
    to_page = lambda a: jnp.pad(jnp.swapaxes(per(a), 1, 2), ((0, 0), (0, 0), (0, LANE - t)))
    y_a = _dsa_sample_call(layer, page_table, iq, iw, qbd, to_page(pr["ik32"]), to_page(pr["ak32"]),
                           to_page(pr["av32"]), ci, ck, cv, t)

    x_new = _finish_call(x2, y_m[:, :t].reshape(b * t, MP), y_a[:, :t].reshape(b * t, ATTN_WIDTH),
                         y_p[:, :t].reshape(b * t, POOL_WIDTH), lw)
    pool_buf = jnp.concatenate([st_pool, per(pr["pu"])], axis=1)[:, -POOL_BUF:]
    return (x_new.reshape(b, t, D_MODEL), pr["ak32"].reshape(b, t, A_HEADS, HEAD_DIM),
            pr["av32"].reshape(b, t, A_HEADS, HEAD_DIM), per(pr["ik32"]), _unpad_state(c_n, m_n), pool_buf)


def kernel(x_prompt, x_sample, cache_k, cache_v, cache_idx_k, state_C, state_n, state_m, state_pool, page_table,
           g_mix, w_in, b_i, b_f, g_q, g_k, g_mhead, w_pool, pool_scale, w_out, g_mlp, w_up, w_down):
    depth = w_in.shape[0]
    bp, s, _ = x_prompt.shape
    db, t, _ = x_sample.shape
    assert bp == 1 and t <= ROWS_T
    n_pool, page = cache_k.shape[1], cache_k.shape[2]
    past_len = page_table.shape[1] * page
    tabs_p = _rope_tables(jnp.arange(s, dtype=jnp.int32))
    tabs_s = _rope_tables(jnp.tile(past_len + jnp.arange(t, dtype=jnp.int32), db))
    ck = jnp.transpose(cache_k, (0, 1, 3, 4, 2)).reshape(depth, n_pool, ATTN_WIDTH, page)
    cv = jnp.transpose(cache_v, (0, 1, 3, 4, 2)).reshape(depth, n_pool, ATTN_WIDTH, page)
    ci = jnp.transpose(cache_idx_k, (0, 1, 3, 2))

    xp, xs = x_prompt[0], x_sample
    outs_p, outs_s = [], []
    for l in range(depth):
        lw = _pack_layer(l, g_mix, w_in, b_i, b_f, g_q, g_k, g_mhead, w_pool, pool_scale, w_out, g_mlp, w_up, w_down)
        xp, *rest_p = _prompt_layer(xp, tabs_p, lw)
        outs_p.append(rest_p)
        xs, *rest_s = _sample_layer(l, xs, tabs_s, lw, state_C[l], state_n[l], state_m[l], state_pool[l],
                                    page_table, ci, ck, cv)
        outs_s.append(rest_s)

    def stack(outs):
        k, v, ik, st, pool = zip(*outs)
        c, n, m = zip(*st)
        return [jnp.stack(a, axis=0) for a in (k, v, ik, c, n, m, pool)]

    return (xp[None], xs, *stack(outs_p), *stack(outs_s))
```

```python
import functools

import numpy as np
import jax
import jax.numpy as jnp
from jax import lax
from jax.experimental import pallas as pl
from jax.experimental.pallas import tpu as pltpu

F32 = jnp.float32
BF16 = jnp.bfloat16

D_MODEL = 1024
M_HEADS = 4
M_DK = 96
A_HEADS = 6
HEAD_DIM = 64
ATTN_WIDTH = A_HEADS * HEAD_DIM
IDX_HEADS = 4
IDX_DIM = 64
TOPK_MAX = 256
ROPE_THETA = 500000.0
ROPE_DIMS = HEAD_DIM // 4
POOL_WINDOWS = (2, 4, 8, 16)
POOL_GDIM = 64
POOL_WIDTH = 256
POOL_BUF = 15
D_FF = 4 * D_MODEL
EPS = 1e-6
NEG = -1e30
MLSTM_WIDTH = M_HEADS * M_DK
SPLIT_SIZES = (MLSTM_WIDTH, MLSTM_WIDTH, MLSTM_WIDTH, MLSTM_WIDTH, M_HEADS, M_HEADS,
               ATTN_WIDTH, ATTN_WIDTH, ATTN_WIDTH, IDX_HEADS * IDX_DIM, IDX_DIM, IDX_HEADS, POOL_WIDTH)
SPLIT_POINTS = tuple(int(s) for s in np.cumsum(SPLIT_SIZES)[:-1])

LANE = 128
MP = M_HEADS * LANE
N_COL = M_DK
OFF_MQ, OFF_MK, OFF_MV, OFF_MO = 0, MP, 2 * MP, 3 * MP
OFF_AQ = 4 * MP
OFF_AK = OFF_AQ + ATTN_WIDTH
OFF_AV = OFF_AK + ATTN_WIDTH
OFF_IQ = OFF_AV + ATTN_WIDTH
OFF_IK = OFF_IQ + IDX_HEADS * LANE
OFF_PU = OFF_IK + LANE
OFF_G = OFF_PU + POOL_WIDTH
W_PACKED = OFF_G + LANE
G_IG, G_LF, G_IW = 0, M_HEADS, 2 * M_HEADS

VMEM_LIMIT = 56 * 1024 * 1024
MAX_SEARCH = 18
MASK_VALUE = -1e30
MAX_SHIFT_BOUND = 40.0
BOUND_SLACK = 1.02

_NT = (((1,), (1,)), ((), ()))


def _cparams(sem):
    return pltpu.CompilerParams(dimension_semantics=sem, vmem_limit_bytes=VMEM_LIMIT)


def _const_spec(shape):
    nd = len(shape)
    return pl.BlockSpec(shape, lambda *_: (0,) * nd, pipeline_mode=pl.Buffered(1))


def _split3(a):
    a1 = a.astype(BF16)
    r1 = a - a1.astype(F32)
    a2 = r1.astype(BF16)
    a3 = (r1 - a2.astype(F32)).astype(BF16)
    return a1, a2, a3


def _dot_exact_lhs(a, b01):
    return sum(jnp.dot(t, b01, preferred_element_type=F32) for t in _split3(a))


def _dot_exact_rhs(a01, b):
    return sum(jnp.dot(a01, t, preferred_element_type=F32) for t in _split3(b))


def _rope_group(x, c, s, period):
    lane = lax.broadcasted_iota(jnp.int32, x.shape, 1)
    first_half = (lane % period) < (ROPE_DIMS // 2)
    partner = jnp.where(first_half, pltpu.roll(x, LANE - ROPE_DIMS // 2, 1), pltpu.roll(x, ROPE_DIMS // 2, 1))
    return x * c + partner * s


def _rope(x, c, s, period):
    groups = [_rope_group(x[:, g * LANE:(g + 1) * LANE], c, s, period) for g in range(x.shape[1] // LANE)]
    return groups[0] if len(groups) == 1 else jnp.concatenate(groups, axis=1)


def _prep_kernel(x_ref, g_ref, w_ref, bias_ref, gq_ref, gk_ref, c64_ref, s64_ref, c128_ref, s128_ref, bd_ref,
                 mq_o, mk_o, mv_o, mo_o, aq_o, ak32_o, akb_o, av32_o, avb_o, iq_o, ik32_o, ikb_o, pu_o,
                 gates_o, gates_t_o):
    x = x_ref[...]
    h = (x * lax.rsqrt(jnp.mean(x * x, axis=-1, keepdims=True) + EPS) * g_ref[...]).astype(BF16)

    def seg(off, width):
        return jnp.dot(h, w_ref[:, off:off + width], preferred_element_type=F32)

    tm = x.shape[0]
    lane_mp = lax.broadcasted_iota(jnp.int32, (tm, MP), 1)
    mq_o[...] = seg(OFF_MQ, MP).astype(BF16)
    mk_o[...] = (seg(OFF_MK, MP) * (M_DK ** -0.5)).astype(BF16)
    mv_o[...] = jnp.where(lane_mp % LANE == N_COL, 1.0, seg(OFF_MV, MP)).astype(BF16)
    mo_o[...] = seg(OFF_MO, MP)

    c64, s64, c128, s128 = c64_ref[...], s64_ref[...], c128_ref[...], s128_ref[...]
    bd = bd_ref[...]

    def qk_norm(z, gain):
        ms = _dot_exact_lhs(z * z, bd) * (1.0 / HEAD_DIM)
        return _rope(z * lax.rsqrt(ms + EPS) * gain, c64, s64, HEAD_DIM)

    aq = qk_norm(seg(OFF_AQ, ATTN_WIDTH), gq_ref[...])
    aq_o[...] = (aq * (HEAD_DIM ** -0.5)).astype(BF16)
    ak = qk_norm(seg(OFF_AK, ATTN_WIDTH), gk_ref[...])
    ak32_o[...] = ak
    akb_o[...] = ak.astype(BF16)
    av = seg(OFF_AV, ATTN_WIDTH)
    av32_o[...] = av
    avb_o[...] = av.astype(BF16)
    iq_o[...] = _rope(seg(OFF_IQ, IDX_HEADS * LANE), c128, s128, LANE).astype(BF16)
    ik = _rope(seg(OFF_IK, LANE), c128, s128, LANE)
    ik32_o[...] = ik[:, :IDX_DIM]
    ikb_o[...] = ik.astype(BF16)
    pu_o[...] = seg(OFF_PU, POOL_WIDTH)

    zg = seg(OFF_G, LANE)
    zb = zg + bias_ref[...]
    lane = lax.broadcasted_iota(jnp.int32, zg.shape, 1)
    log_sig = jnp.minimum(zb, 0.0) - jnp.log1p(jnp.exp(-jnp.abs(zb)))
    iw_scale = (IDX_HEADS ** -0.5) * (IDX_DIM ** -0.5)
    gates = jnp.where(lane < G_LF, zb,
                      jnp.where(lane < G_IW, log_sig,
                                jnp.where(lane < G_IW + IDX_HEADS, zg * iw_scale, 0.0)))
    gates_o[...] = gates
    gates_t_o[...] = gates.T[:8, :]


def _prep_call(x, pos_tabs, lw):
    t = x.shape[0]
    tm = min(256, t)
    row = lambda w: pl.BlockSpec((tm, w), lambda i: (i, 0))
    out_shapes = dict(
        mq=(MP, BF16), mk=(MP, BF16), mv=(MP, BF16), mo=(MP, F32),
        aq=(ATTN_WIDTH, BF16), ak32=(ATTN_WIDTH, F32), akb=(ATTN_WIDTH, BF16),
        av32=(ATTN_WIDTH, F32), avb=(ATTN_WIDTH, BF16),
        iq=(IDX_HEADS * LANE, BF16), ik32=(IDX_DIM, F32), ikb=(LANE, BF16),
        pu=(POOL_WIDTH, F32), gates=(LANE, F32))
    names = list(out_shapes)
    out_shape = [jax.ShapeDtypeStruct((t, w), dt) for w, dt in out_shapes.values()]
    out_specs = [row(w) for w, _ in out_shapes.values()]
    out_shape.append(jax.ShapeDtypeStruct((8, t), F32))
    out_specs.append(pl.BlockSpec((8, tm), lambda i: (0, i)))
    outs = pl.pallas_call(
        _prep_kernel,
        grid=(t // tm,),
        in_specs=[row(D_MODEL), _const_spec((1, D_MODEL)), _const_spec((D_MODEL, W_PACKED)),
                  _const_spec((1, LANE)), _const_spec((1, ATTN_WIDTH)), _const_spec((1, ATTN_WIDTH)),
                  row(LANE), row(LANE), row(LANE), row(LANE), _const_spec((ATTN_WIDTH, ATTN_WIDTH))],
        out_specs=out_specs,
        out_shape=out_shape,
        compiler_params=_cparams(("arbitrary",)),
        name="prep",
    )(x, lw["g_mix"], lw["w_in"], lw["gate_bias"], lw["g_q"], lw["g_k"], *pos_tabs, lw["bd64"])
    res = dict(zip(names, outs[:-1]))
    res["gates_t"] = outs[-1]
    return res


def _mlstm_kernel(q_ref, k_ref, v_ref, mo_ref, g_ref, gt_ref, gm_ref, c0_ref, m0_ref,
                  y_ref, cn_ref, mn_ref, c_scr, m_scr):
    j = pl.program_id(1)

    @pl.when(j == 0)
    def _():
        c_scr[...] = c0_ref[0]
        m_scr[...] = m0_ref[0]

    l = q_ref.shape[1]
    g = g_ref[0]
    gt = gt_ref[0]
    r_i = lax.broadcasted_iota(jnp.int32, (l, l), 0)
    c_i = lax.broadcasted_iota(jnp.int32, (l, l), 1)
    causal = c_i <= r_i
    tri_l = jnp.where(causal, 1.0, 0.0).astype(BF16)
    tri_u = jnp.where(r_i <= c_i, 1.0, 0.0).astype(BF16)
    bcum_c = _dot_exact_rhs(tri_l, g)
    bcum_r = _dot_exact_lhs(gt, tri_u)
    lane = lax.broadcasted_iota(jnp.int32, (l, LANE), 1)

    for h in range(M_HEADS):
        sl = slice(h * LANE, (h + 1) * LANE)
        qh, kh, vh = q_ref[0, :, sl], k_ref[0, :, sl], v_ref[0, :, sl]
        bc = bcum_c[:, G_LF + h:G_LF + h + 1]
        br = bcum_r[G_LF + h:G_LF + h + 1, :]
        ig_r = gt[G_IG + h:G_IG + h + 1, :]
        ig_c = g[:, G_IG + h:G_IG + h + 1]
        m_prev = m_scr[h][:, :1]
        c_prev = c_scr[h]

        log_d = jnp.where(causal, bc - br + ig_r, -jnp.inf)
        inter = bc + m_prev
        m_t = jnp.maximum(inter, jnp.max(log_d, axis=1, keepdims=True))
        s = lax.dot_general(qh, kh, _NT, preferred_element_type=F32) * jnp.exp(log_d - m_t)
        inter_w = jnp.exp(inter - m_t)
        num = (jnp.dot(s.astype(BF16), vh, preferred_element_type=F32)
               + inter_w * jnp.dot(qh, c_prev.astype(BF16), preferred_element_type=F32))
        den = num[:, N_COL:N_COL + 1]
        hval = num / jnp.maximum(jnp.abs(den), jnp.exp(-m_t))

        b_last = bc[l - 1:l, :]
        log_w = b_last - bc + ig_c
        m_new = jnp.maximum(b_last + m_prev, jnp.max(log_w, axis=0, keepdims=True))
        w_exp = jnp.exp(log_w - m_new)
        decay = jnp.exp(b_last + m_prev - m_new)
        kw_t = (kh.astype(F32) * w_exp).T.astype(BF16)
        c_scr[h] = decay * c_prev + jnp.dot(kw_t, vh, preferred_element_type=F32)
        m_scr[h] = jnp.broadcast_to(m_new, (1, LANE))

        o = jnp.where(lane < M_DK, jax.nn.sigmoid(mo_ref[0, :, sl]) * hval, 0.0)
        ms = jnp.sum(o * o, axis=1, keepdims=True) * (1.0 / M_DK)
        y_ref[0, :, sl] = (o * lax.rsqrt(ms + EPS) * gm_ref[:, sl]).astype(BF16)

    @pl.when(j == pl.num_programs(1) - 1)
    def _():
        cn_ref[0] = c_scr[...]
        mn_ref[0] = m_scr[...]


def _mlstm_call(q, k, v, mo, gates, gates_t, gm, c0, m0, chunk):
    b, t, _ = q.shape
    tok = lambda w: pl.BlockSpec((1, chunk, w), lambda i, j: (i, j, 0))
    st_c = pl.BlockSpec((1, M_HEADS, LANE, LANE), lambda i, j: (i, 0, 0, 0))
    st_m = pl.BlockSpec((1, M_HEADS, 1, LANE), lambda i, j: (i, 0, 0, 0))
    return pl.pallas_call(
        _mlstm_kernel,
        grid=(b, t // chunk),
        in_specs=[tok(MP), tok(MP), tok(MP), tok(MP), tok(LANE),
                  pl.BlockSpec((1, 8, chunk), lambda i, j: (i, 0, j)),
                  pl.BlockSpec((1, MP), lambda i, j: (0, 0)), st_c, st_m],
        out_specs=[tok(MP), st_c, st_m],
        out_shape=[jax.ShapeDtypeStruct((b, t, MP), BF16),
                   jax.ShapeDtypeStruct((b, M_HEADS, LANE, LANE), F32),
                   jax.ShapeDtypeStruct((b, M_HEADS, 1, LANE), F32)],
        scratch_shapes=[pltpu.VMEM((M_HEADS, LANE, LANE), F32), pltpu.VMEM((M_HEADS, 1, LANE), F32)],
        compiler_params=_cparams(("arbitrary", "arbitrary")),
        name="mlstm",
    )(q, k, v, mo, gates, gates_t, gm, c0, m0)


_HALO = 16


def _pool_kernel(u_ref, hist_ref, w_ref, sc_ref, y_ref, ext, *, pos0, carry):
    j = pl.program_id(1)
    tm = u_ref.shape[1]

    @pl.when(j == 0)
    def _():
        ext[0:_HALO, :] = hist_ref[0]

    ext[_HALO:_HALO + tm, :] = u_ref[0]
    cur = ext[_HALO:_HALO + tm, :]
    pos = pos0 + j * tm + lax.broadcasted_iota(jnp.int32, (tm, 1), 0)
    lane = lax.broadcasted_iota(jnp.int32, (tm, POOL_WIDTH), 1)
    run = cur
    d = None
    back = 1
    for gi, w in enumerate(POOL_WINDOWS):
        while back < w:
            run = run + ext[_HALO - back:_HALO - back + tm, :]
            back += 1
        cnt = jnp.minimum(w, pos + 1).astype(F32)
        dg = run / cnt - cur
        d = dg if d is None else jnp.where(lane >= gi * POOL_GDIM, dg, d)
    y = jnp.dot(d.astype(BF16), w_ref[...], preferred_element_type=F32) * sc_ref[...]
    y_ref[0] = y.astype(BF16)
    if carry:
        ext[0:_HALO, :] = ext[tm:tm + _HALO, :]


def _pool_call(u, hist, w_bd, scale, pos0):
    b, t, _ = u.shape
    tm = min(512, t)
    nt = t // tm
    return pl.pallas_call(
        functools.partial(_pool_kernel, pos0=pos0, carry=nt > 1),
        grid=(b, nt),
        in_specs=[pl.BlockSpec((1, tm, POOL_WIDTH), lambda i, j: (i, j, 0)),
                  pl.BlockSpec((1, _HALO, POOL_WIDTH), lambda i, j: (i, 0, 0)),
                  pl.BlockSpec((POOL_WIDTH, POOL_WIDTH), lambda i, j: (0, 0)),
                  pl.BlockSpec((1, POOL_WIDTH), lambda i, j: (0, 0))],
        out_specs=pl.BlockSpec((1, tm, POOL_WIDTH), lambda i, j: (i, j, 0)),
        out_shape=jax.ShapeDtypeStruct((b, t, POOL_WIDTH), BF16),
        scratch_shapes=[pltpu.VMEM((tm + _HALO, POOL_WIDTH), F32)],
        compiler_params=_cparams(("arbitrary", "arbitrary")),
        name="pool",
    )(u, hist, w_bd, scale)


_FF_CHUNK = 1024


def _finish_kernel(x_ref, mm_ref, ma_ref, mp_ref, wo_ref, g_ref, wu_ref, wd_ref, y_ref):
    o_a, o_p = MP, MP + ATTN_WIDTH
    xr = (x_ref[...]
          + jnp.dot(mm_ref[...], wo_ref[0:o_a, :], preferred_element_type=F32)
          + jnp.dot(ma_ref[...], wo_ref[o_a:o_p, :], preferred_element_type=F32)
          + jnp.dot(mp_ref[...], wo_ref[o_p:o_p + POOL_WIDTH, :], preferred_element_type=F32))
    hn = (xr * lax.rsqrt(jnp.mean(xr * xr, axis=-1, keepdims=True) + EPS) * g_ref[...]).astype(BF16)
    mlp = None
    for c in range(D_FF // _FF_CHUNK):
        sl = slice(c * _FF_CHUNK, (c + 1) * _FF_CHUNK)
        hid = jnp.maximum(jnp.dot(hn, wu_ref[:, sl], preferred_element_type=F32), 0.0)
        d = jnp.dot((hid * hid).astype(BF16), wd_ref[sl, :], preferred_element_type=F32)
        mlp = d if mlp is None else mlp + d
    y_ref[...] = xr + mlp


def _finish_call(x, mix_m, mix_a, mix_p, lw):
    t = x.shape[0]
    tm = min(512, t)
    row = lambda w: pl.BlockSpec((tm, w), lambda i: (i, 0))
    return pl.pallas_call(
        _finish_kernel,
        grid=(t // tm,),
        in_specs=[row(D_MODEL), row(MP), row(ATTN_WIDTH), row(POOL_WIDTH),
                  _const_spec((MP + ATTN_WIDTH + POOL_WIDTH, D_MODEL)), _const_spec((1, D_MODEL)),
                  _const_spec((D_MODEL, D_FF)), _const_spec((D_FF, D_MODEL))],
        out_specs=row(D_MODEL),
        out_shape=jax.ShapeDtypeStruct((t, D_MODEL), F32),
        compiler_params=_cparams(("arbitrary",)),
        name="finish",
    )(x, mix_m, mix_a, mix_p, lw["w_out"], lw["g_mlp"], lw["w_up"], lw["w_down"])


_BIG_SLOTS = 1e9


def _row_max(x):
    return jnp.max(x, axis=1, keepdims=True)


def _select_threshold(count_ge, max_below, mn, mx, n_adm, n_inadm, topk):
    kf = float(topk)
    g0 = n_adm + jnp.where(mn <= NEG, n_inadm, 0.0)
    all_kept = g0 < kf
    top = jnp.maximum(mx, NEG)
    hi0 = top + (jnp.abs(top) + 1.0) * 2.0 ** -10

    def search_cond(c):
        unsettled = jnp.where(all_kept | (c[3] == kf), 0.0, 1.0)
        return (c[0] < MAX_SEARCH) & (jnp.max(unsettled) > 0.5)

    def search(c):
        it, lo, hi, clo, chi = c
        mid = lo + (hi - lo) * 0.5
        cm = count_ge(mid)
        ge = cm >= kf
        return (it + 1, jnp.where(ge, mid, lo), jnp.where(ge, hi, mid),
                jnp.where(ge, cm, clo), jnp.where(ge, chi, cm))

    _, lo, hi, clo, chi = lax.while_loop(search_cond, search, (jnp.int32(0), mn, hi0, g0, jnp.zeros_like(mn)))
    thr0 = jnp.where(all_kept, mn, lo)
    done0 = jnp.where(all_kept | (clo == kf), 1.0, 0.0)

    def cond(c):
        return (c[0] <= topk) & (jnp.min(c[5]) < 0.5)

    def peel(c):
        it, (hi, chi, thr, slots, done, tie) = c[0], c[1:]
        v = max_below(hi)
        cv = count_ge(v)
        fin = (cv >= kf) & (done < 0.5)
        thr = jnp.where(fin, v, thr)
        slots = jnp.where(fin, kf - chi, slots)
        tie = jnp.where(fin & (cv > kf), 1.0, tie)
        done = jnp.where(fin, 1.0, done)
        live = done < 0.5
        return (it + 1, jnp.where(live, v, hi), jnp.where(live, cv, chi), thr, slots, done, tie)

    init = (jnp.int32(0), hi, chi, thr0, jnp.full_like(mn, _BIG_SLOTS), done0, jnp.zeros_like(mn))
    _, _, _, thr, slots, _, tie = lax.while_loop(cond, peel, init)
    return thr, slots, jnp.max(tie) > 0.5


KEY_GROUP = 2
_SUB = 8


def _dsa_prompt_kernel(iq_ref, aq_ref, g_ref, bound_ref, ik_ref, ak_ref, av_ref, o_ref,
                       s_scr, t_scr, iw_scr, qm_scr, mb_scr, acc_scr, m_scr, l_scr, *, tq, n_total, topk):
    i = pl.program_id(0)
    tk = tq
    nch = tk // LANE
    nrb = tq // _SUB
    grp = KEY_GROUP
    nkb = (i + grp) // grp
    q_pos = i * tq + lax.broadcasted_iota(jnp.int32, (tq, 1), 0)
    gates = g_ref[...]
    for h in range(IDX_HEADS):
        iw_scr[h] = jnp.broadcast_to(gates[:, G_IW + h:G_IW + h + 1], (tq, LANE))

    def tile_scores(kt):
        kk = ik_ref[pl.ds(pl.multiple_of(kt * tk, tk), tk), :]
        s = None
        for h in range(IDX_HEADS):
            sc = lax.dot_general(iq_ref[:, h * LANE:(h + 1) * LANE], kk, _NT, preferred_element_type=F32)
            w = iw_scr[h]
            term = jnp.concatenate([jnp.maximum(sc[:, c * LANE:(c + 1) * LANE], 0.0) * w for c in range(nch)], axis=1)
            s = term if s is None else s + term
        return s

    def fold(mx, mn, hi_side, lo_side):
        for c in range(nch):
            mx = jnp.maximum(mx, hi_side[:, c * LANE:(c + 1) * LANE])
            mn = jnp.minimum(mn, lo_side[:, c * LANE:(c + 1) * LANE])
        return mx, mn

    def score_block(kb, carry):
        mx, mn = carry
        for j in range(grp):
            kt = kb * grp + j
            s = tile_scores(kt)
            adm = (kt * tk + lax.broadcasted_iota(jnp.int32, (1, tk), 1)) <= q_pos
            s_masked = jnp.where(adm, s, -jnp.inf)
            s_scr[kt] = s_masked
            mx, mn = fold(mx, mn, s_masked, jnp.where(adm, s, jnp.inf))
        return mx, mn

    mx, mn = lax.fori_loop(0, nkb, score_block,
                           (jnp.full((tq, LANE), -jnp.inf, F32), jnp.full((tq, LANE), jnp.inf, F32)))
    mx = _row_max(mx)
    mn = jnp.min(mn, axis=1, keepdims=True)
    n_adm = (q_pos + 1).astype(F32)
    n_inadm = float(n_total) - n_adm

    def scan_scores(t, init, step):
        t_scr[...] = jnp.broadcast_to(t, (tq, LANE))

        def body(kb, accs):
            out = []
            for rb in range(nrb):
                rows = slice(rb * _SUB, (rb + 1) * _SUB)
                tb = t_scr[rows, :]
                a = accs[rb]
                for j in range(grp):
                    for c in range(nch):
                        a = step(a, s_scr[kb * grp + j, rows, c * LANE:(c + 1) * LANE], tb)
                out.append(a)
            return tuple(out)

        accs = lax.fori_loop(0, nkb, body, tuple(jnp.full((_SUB, LANE), init, F32) for _ in range(nrb)))
        return jnp.concatenate(accs, axis=0)

    def count_ge(t):
        acc = scan_scores(t, 0.0, lambda a, sv, tb: a + jnp.where(sv >= tb, 1.0, 0.0))
        return jnp.sum(acc, axis=1, keepdims=True) + jnp.where(t <= NEG, n_inadm, 0.0)

    def max_below(hi):
        acc = scan_scores(hi, -jnp.inf, lambda a, sv, tb: jnp.maximum(a, jnp.where(sv < tb, sv, -jnp.inf)))
        tail = jnp.where((hi > NEG) & (n_inadm > 0.0), NEG, -jnp.inf)
        return jnp.maximum(_row_max(acc), tail)

    thr, slots, any_tie = _select_threshold(count_ge, max_below, mn, mx, n_adm, n_inadm, topk)

    lane = lax.broadcasted_iota(jnp.int32, (tq, LANE), 1)
    for h in range(A_HEADS):
        qp = aq_ref[:, (h // 2) * LANE:(h // 2 + 1) * LANE].astype(F32)
        keep = (lane < HEAD_DIM) if h % 2 == 0 else (lane >= HEAD_DIM)
        qm_scr[h] = jnp.where(keep, qp, 0.0).astype(BF16)
    acc_scr[...] = jnp.zeros_like(acc_scr)
    l_scr[...] = jnp.zeros_like(l_scr)
    m_scr[...] = jnp.full_like(m_scr, MASK_VALUE)
    thr_b = jnp.broadcast_to(thr, (tq, LANE))

    tkb = grp * tk

    def attend(kb, kept):
        rows = pl.ds(pl.multiple_of(kb * tkb, tkb), tkb)
        for p in range(A_HEADS // 2):
            kp = ak_ref[rows, p * LANE:(p + 1) * LANE]
            vp = av_ref[rows, p * LANE:(p + 1) * LANE]
            for h in (2 * p, 2 * p + 1):
                lg = lax.dot_general(qm_scr[h], kp, _NT, preferred_element_type=F32)
                lgm = [jnp.where(kept[c], lg[:, c * LANE:(c + 1) * LANE], MASK_VALUE) for c in range(grp * nch)]
                cm = lgm[0]
                for x in lgm[1:]:
                    cm = jnp.maximum(cm, x)
                m_old = m_scr[h]
                m_new = jnp.maximum(m_old, _row_max(cm))
                alpha = jnp.exp(m_old - m_new)
                pc = [jnp.exp(x - m_new) for x in lgm]
                l_scr[h] = alpha * l_scr[h] + sum(pc)
                pmat = jnp.concatenate(pc, axis=1).astype(BF16)
                acc_scr[h] = alpha * acc_scr[h] + jnp.dot(pmat, vp, preferred_element_type=F32)
                m_scr[h] = m_new

    bound = bound_ref[...]
    neg_b = jnp.broadcast_to(-bound, (tq, LANE))

    r_i = lax.broadcasted_iota(jnp.int32, (tk, tk), 0)
    c_i = lax.broadcasted_iota(jnp.int32, (tk, tk), 1)
    slots_l = jnp.broadcast_to(slots, (tq, LANE))

    def bounded_path(tie_aware):
        def body(kb, seen):
            for j in range(grp):
                sv = s_scr[kb * grp + j]
                svc = [sv[:, c * LANE:(c + 1) * LANE] for c in range(nch)]
                if tie_aware:
                    before = jnp.where(r_i < c_i, 1.0, 0.0).astype(BF16)
                    eqc = [jnp.where(x == thr_b, 1.0, 0.0) for x in svc]
                    prior = jnp.dot(jnp.concatenate(eqc, axis=1).astype(BF16), before, preferred_element_type=F32)
                    mbc = [jnp.where(x > thr_b, neg_b,
                                     jnp.where(prior[:, c * LANE:(c + 1) * LANE] + seen < slots_l,
                                               jnp.where(x == thr_b, neg_b, -jnp.inf), -jnp.inf))
                           for c, x in enumerate(svc)]
                    seen = seen + jnp.sum(sum(eqc), axis=1, keepdims=True)
                else:
                    mbc = [jnp.where(x >= thr_b, neg_b, -jnp.inf) for x in svc]
                for c in range(nch):
                    cc = j * nch + c
                    mb_scr[:, cc * LANE:(cc + 1) * LANE] = mbc[c]
            rows = pl.ds(pl.multiple_of(kb * tkb, tkb), tkb)
            for p in range(A_HEADS // 2):
                kp = ak_ref[rows, p * LANE:(p + 1) * LANE]
                vp = av_ref[rows, p * LANE:(p + 1) * LANE]
                for h in (2 * p, 2 * p + 1):
                    lg = lax.dot_general(qm_scr[h], kp, _NT, preferred_element_type=F32)
                    pc = [jnp.exp(lg[:, c * LANE:(c + 1) * LANE] + mb_scr[:, c * LANE:(c + 1) * LANE])
                          for c in range(grp * nch)]
                    l_scr[h] = l_scr[h] + sum(pc)
                    pmat = jnp.concatenate(pc, axis=1).astype(BF16)
                    acc_scr[h] = acc_scr[h] + jnp.dot(pmat, vp, preferred_element_type=F32)
            return seen
        lax.fori_loop(0, nkb, body, jnp.zeros((tq, LANE), F32))

    def general_path():
        before = jnp.where(r_i < c_i, 1.0, 0.0).astype(BF16)
        slots_b = jnp.broadcast_to(slots, (tq, tk))
        thr_t = jnp.broadcast_to(thr, (tq, tk))

        def body(kb, seen):
            kept = []
            for j in range(grp):
                sv = s_scr[kb * grp + j]
                eq = jnp.where(sv == thr_t, 1.0, 0.0)
                prior = jnp.dot(eq.astype(BF16), before, preferred_element_type=F32) + seen
                kept_f = jnp.where(sv > thr_t, 1.0, jnp.where(prior < slots_b, eq, 0.0))
                kept += [kept_f[:, c * LANE:(c + 1) * LANE] > 0.5 for c in range(nch)]
                seen = seen + jnp.sum(eq, axis=1, keepdims=True)
            attend(kb, kept)
            return seen
        lax.fori_loop(0, nkb, body, jnp.zeros((tq, 1), F32))

    lax.cond(jnp.max(bound) <= MAX_SHIFT_BOUND,
             lambda: lax.cond(any_tie, lambda: bounded_path(True), lambda: bounded_path(False)),
             general_path)

    for p in range(A_HEADS // 2):
        o0 = acc_scr[2 * p] / jnp.sum(l_scr[2 * p], axis=1, keepdims=True)
        o1 = acc_scr[2 * p + 1] / jnp.sum(l_scr[2 * p + 1], axis=1, keepdims=True)
        o_ref[:, p * LANE:(p + 1) * LANE] = jnp.where(lane < HEAD_DIM, o0, o1).astype(BF16)


def _dsa_prompt_call(iq, aq, gates, bound, ikb, akb, avb):
    t = iq.shape[0]
    tq = min(256, t)
    assert (t // tq) % KEY_GROUP == 0
    topk = min(TOPK_MAX, t // 4)
    row = lambda w: pl.BlockSpec((tq, w), lambda i: (i, 0))
    return pl.pallas_call(
        functools.partial(_dsa_prompt_kernel, tq=tq, n_total=t, topk=topk),
        grid=(t // tq,),
        in_specs=[row(IDX_HEADS * LANE), row(ATTN_WIDTH), row(LANE), _const_spec((1, LANE)),
                  _const_spec((t, LANE)), _const_spec((t, ATTN_WIDTH)), _const_spec((t, ATTN_WIDTH))],
        out_specs=row(ATTN_WIDTH),
        out_shape=jax.ShapeDtypeStruct((t, ATTN_WIDTH), BF16),
        scratch_shapes=[pltpu.VMEM((t // tq, tq, tq), F32),
                        pltpu.VMEM((tq, LANE), F32),
                        pltpu.VMEM((IDX_HEADS, tq, LANE), F32),
                        pltpu.VMEM((A_HEADS, tq, LANE), BF16),
                        pltpu.VMEM((tq, KEY_GROUP * tq), F32),
                        pltpu.VMEM((A_HEADS, tq, LANE), F32),
                        pltpu.VMEM((A_HEADS, tq, LANE), F32),
                        pltpu.VMEM((A_HEADS, tq, LANE), F32)],
        compiler_params=_cparams(("arbitrary",)),
        name="dsa_prompt",
    )(iq, aq, gates, bound, ikb, akb, avb)


ROWS_T = 8
PAGES_PER_DMA = 16


def _dsa_sample_kernel(pt_ref, iq_ref, iw_ref, aq_ref, ikn_ref, kn_ref, vn_ref, ci_ref, ck_ref, cv_ref, o_ref,
                       s_scr, ibuf, kbuf, vbuf, isem, ksem, vsem, acc_scr, m_scr, l_scr,
                       *, layer, n_pages, n_new, past_len, topk):
    b = pl.program_id(0)
    page = LANE
    pg = PAGES_PER_DMA
    n_chunks = n_pages // pg
    a_rows = A_HEADS * ROWS_T

    def idx_copies(c, slot):
        return [pltpu.make_async_copy(ci_ref.at[layer, pt_ref[b, c * pg + p]], ibuf.at[slot, p], isem.at[slot])
                for p in range(pg)]

    def kv_copies(c, slot):
        cps = []
        for p in range(pg):
            phys = pt_ref[b, c * pg + p]
            cps.append(pltpu.make_async_copy(ck_ref.at[layer, phys], kbuf.at[slot, p], ksem.at[slot]))
            cps.append(pltpu.make_async_copy(cv_ref.at[layer, phys], vbuf.at[slot, p], vsem.at[slot]))
        return cps

    tok = lax.broadcasted_iota(jnp.int32, (ROWS_T, 1), 0)
    slot_i = lax.broadcasted_iota(jnp.int32, (1, page), 1)
    new_adm = (slot_i <= tok) & (slot_i < n_new)
    iq = iq_ref[0]
    iw = iw_ref[0]

    def page_scores(page_f32):
        sc = jnp.dot(iq, page_f32.astype(BF16), preferred_element_type=F32)
        r = jnp.maximum(sc, 0.0) * iw
        s = r[0:ROWS_T]
        for h in range(1, IDX_HEADS):
            s = s + r[h * ROWS_T:(h + 1) * ROWS_T]
        return s

    for cp in idx_copies(0, 0):
        cp.start()

    def score_chunk(c, carry):
        mx, mn = carry
        slot = c % 2

        @pl.when(c + 1 < n_chunks)
        def _():
            for cp in idx_copies(c + 1, 1 - slot):
                cp.start()

        for cp in idx_copies(c, slot):
            cp.wait()
        for p in range(pg):
            s = page_scores(ibuf[slot, p])
            s_scr[c * pg + p] = s
            mx = jnp.maximum(mx, s)
            mn = jnp.minimum(mn, s)
        return mx, mn

    mx, mn = lax.fori_loop(0, n_chunks, score_chunk,
                           (jnp.full((ROWS_T, page), -jnp.inf, F32), jnp.full((ROWS_T, page), jnp.inf, F32)))
    s_new = page_scores(ikn_ref[0])
    s_scr[n_pages] = jnp.where(new_adm, s_new, -jnp.inf)
    mx = _row_max(jnp.maximum(mx, jnp.where(new_adm, s_new, -jnp.inf)))
    mn = jnp.min(jnp.minimum(mn, jnp.where(new_adm, s_new, jnp.inf)), axis=1, keepdims=True)

    real = tok < n_new
    n_adm = jnp.where(real, (past_len + tok + 1).astype(F32), 0.0)
    n_inadm = jnp.where(real, (n_new - 1 - tok).astype(F32), 0.0)

    for cp in kv_copies(0, 0):
        cp.start()

    def scan_pages(init, step):
        lanes = [jnp.full((ROWS_T, page), init, F32) for _ in range(4)]
        for j in range(n_pages + 1):
            lanes[j % 4] = step(lanes[j % 4], s_scr[j])
        return lanes

    def count_ge(t):
        tb = jnp.broadcast_to(t, (ROWS_T, page))
        parts = scan_pages(0.0, lambda a, sv: a + jnp.where(sv >= tb, 1.0, 0.0))
        acc = (parts[0] + parts[1]) + (parts[2] + parts[3])
        return jnp.sum(acc, axis=1, keepdims=True) + jnp.where(t <= NEG, n_inadm, 0.0)

    def max_below(hi):
        hb = jnp.broadcast_to(hi, (ROWS_T, page))
        parts = scan_pages(-jnp.inf, lambda a, sv: jnp.maximum(a, jnp.where(sv < hb, sv, -jnp.inf)))
        acc = jnp.maximum(jnp.maximum(parts[0], parts[1]), jnp.maximum(parts[2], parts[3]))
        tail = jnp.where((hi > NEG) & (n_inadm > 0.0), NEG, -jnp.inf)
        return jnp.maximum(_row_max(acc), tail)

    thr, slots, any_tie = _select_threshold(count_ge, max_below, mn, mx, n_adm, n_inadm, topk)

    acc_scr[...] = jnp.zeros_like(acc_scr)
    l_scr[...] = jnp.zeros_like(l_scr)
    m_scr[...] = jnp.full_like(m_scr, MASK_VALUE)
    qbd = aq_ref[0]
    thr_b = jnp.broadcast_to(thr, (ROWS_T, page))
    slots_b = jnp.broadcast_to(slots, (ROWS_T, page))
    r_i = lax.broadcasted_iota(jnp.int32, (page, page), 0)
    c_i = lax.broadcasted_iota(jnp.int32, (page, page), 1)
    before = jnp.where(r_i < c_i, 1.0, 0.0).astype(BF16)

    def kept_masks(pages, seen):
        svs = [s_scr[j] for j in pages]

        def plain():
            return [jnp.where(sv >= thr_b, 1.0, 0.0) for sv in svs], seen

        def tie_aware():
            eqs = [jnp.where(sv == thr_b, 1.0, 0.0) for sv in svs]
            local = [jnp.dot(eq.astype(BF16), before, preferred_element_type=F32) for eq in eqs]
            counts = [jnp.sum(eq, axis=1, keepdims=True) for eq in eqs]
            run, out = seen, []
            for sv, eq, loc, cnt in zip(svs, eqs, local, counts):
                out.append(jnp.where(sv > thr_b, 1.0, jnp.where(loc + run < slots_b, eq, 0.0)))
                run = run + cnt
            return out, run

        return lax.cond(any_tie, tie_aware, plain)

    def attend(k_pages, v_pages, kept):
        lgm = []
        for kt, km in zip(k_pages, kept):
            lg = jnp.dot(qbd, kt.astype(BF16), preferred_element_type=F32)
            km_all = jnp.concatenate([km] * A_HEADS, axis=0)
            lgm.append(jnp.where(km_all > 0.5, lg, MASK_VALUE))
        cm = lgm[0]
        for x in lgm[1:]:
            cm = jnp.maximum(cm, x)
        m_old = m_scr[...]
        m_new = jnp.maximum(m_old, _row_max(cm))
        alpha = jnp.exp(m_old - m_new)
        pc = [jnp.exp(x - m_new) for x in lgm]
        l_scr[...] = alpha * l_scr[...] + sum(pc)
        pv = None
        for p_t, vt in zip(pc, v_pages):
            d = lax.dot_general(p_t.astype(BF16), vt.astype(BF16), _NT, preferred_element_type=F32)
            pv = d if pv is None else pv + d
        acc_scr[...] = alpha[:, :1] * acc_scr[...] + pv
        m_scr[...] = m_new

    def chunk(c, seen):
        slot = c % 2

        @pl.when(c + 1 < n_chunks)
        def _():
            for cp in kv_copies(c + 1, 1 - slot):
                cp.start()

        for cp in kv_copies(c, slot):
            cp.wait()
        kept, seen = kept_masks([c * pg + p for p in range(pg)], seen)
        attend([kbuf[slot, p] for p in range(pg)], [vbuf[slot, p] for p in range(pg)], kept)
        return seen

    seen = lax.fori_loop(0, n_chunks, chunk, jnp.zeros((ROWS_T, 1), F32))
    kept, _ = kept_masks([n_pages], seen)
    attend([kn_ref[0]], [vn_ref[0]], kept)

    out_rows = acc_scr[...] / jnp.sum(l_scr[...], axis=1, keepdims=True)
    lane = lax.broadcasted_iota(jnp.int32, (ROWS_T, ATTN_WIDTH), 1)
    out = jnp.zeros((ROWS_T, ATTN_WIDTH), F32)
    for h in range(A_HEADS):
        out = jnp.where(lane // HEAD_DIM == h, out_rows[h * ROWS_T:(h + 1) * ROWS_T], out)
    o_ref[0] = out.astype(BF16)


def _dsa_sample_call(layer, page_table, iq, iw, aq, ik_new, k_new, v_new, ci, ck, cv, n_new):
    b, n_pages = page_table.shape
    page = ci.shape[-1]
    assert page == LANE and n_pages % PAGES_PER_DMA == 0
    past_len = n_pages * page
    topk = min(TOPK_MAX, (past_len + n_new) // 4)
    a_rows = A_HEADS * ROWS_T
    bspec = lambda r, w: pl.BlockSpec((1, r, w), lambda i, pt: (i, 0, 0))
    hbm = pl.BlockSpec(memory_space=pl.ANY)
    pg = PAGES_PER_DMA
    return pl.pallas_call(
        functools.partial(_dsa_sample_kernel, layer=layer, n_pages=n_pages, n_new=n_new, past_len=past_len, topk=topk),
        grid_spec=pltpu.PrefetchScalarGridSpec(
            num_scalar_prefetch=1,
            grid=(b,),
            in_specs=[bspec(IDX_HEADS * ROWS_T, IDX_DIM), bspec(IDX_HEADS * ROWS_T, LANE), bspec(a_rows, ATTN_WIDTH),
                      bspec(IDX_DIM, page), bspec(ATTN_WIDTH, page), bspec(ATTN_WIDTH, page), hbm, hbm, hbm],
            out_specs=bspec(ROWS_T, ATTN_WIDTH),
            scratch_shapes=[pltpu.VMEM((n_pages + 1, ROWS_T, page), F32),
                            pltpu.VMEM((2, pg, IDX_DIM, page), F32),
                            pltpu.VMEM((2, pg, ATTN_WIDTH, page), F32),
                            pltpu.VMEM((2, pg, ATTN_WIDTH, page), F32),
                            pltpu.SemaphoreType.DMA((2,)), pltpu.SemaphoreType.DMA((2,)), pltpu.SemaphoreType.DMA((2,)),
                            pltpu.VMEM((a_rows, ATTN_WIDTH), F32),
                            pltpu.VMEM((a_rows, page), F32),
                            pltpu.VMEM((a_rows, page), F32)]),
        out_shape=jax.ShapeDtypeStruct((b, ROWS_T, ATTN_WIDTH), BF16),
        compiler_params=_cparams(("arbitrary",)),
        name="dsa_sample",
    )(page_table, iq, iw, aq, ik_new, k_new, v_new, ci, ck, cv)


def _pad_heads(a, n_heads, head_dim):
    r = a.shape[0]
    a = a.reshape(r, n_heads, head_dim)
    return jnp.pad(a, ((0, 0), (0, 0), (0, LANE - head_dim))).reshape(r, n_heads * LANE)


def _pack_layer(l, g_mix, w_in, b_i, b_f, g_q, g_k, g_mhead, w_pool, pool_scale, w_out, g_mlp, w_up, w_down):
    mq, mk, mv, mo, mi, mf, aq, ak, av, iq, ik, iw, pu = jnp.split(w_in[l], SPLIT_POINTS, axis=1)
    gate_w = jnp.pad(jnp.concatenate([mi, mf, iw], axis=1), ((0, 0), (0, LANE - 3 * M_HEADS)))
    w_packed = jnp.concatenate(
        [_pad_heads(mq, M_HEADS, M_DK), _pad_heads(mk, M_HEADS, M_DK), _pad_heads(mv, M_HEADS, M_DK),
         _pad_heads(mo, M_HEADS, M_DK), aq, ak, av, _pad_heads(iq, IDX_HEADS, IDX_DIM),
         jnp.pad(ik, ((0, 0), (0, LANE - IDX_DIM))), pu, gate_w], axis=1).astype(BF16)
    gate_bias = jnp.pad(jnp.concatenate([b_i[l], b_f[l]]), (0, LANE - 2 * M_HEADS)).reshape(1, LANE)
    head_id = np.arange(ATTN_WIDTH) // HEAD_DIM
    bd64 = jnp.asarray(head_id[:, None] == head_id[None, :], BF16)
    grp = np.arange(POOL_WIDTH) // POOL_GDIM
    w_bd = jnp.where(jnp.asarray(grp[:, None] == grp[None, :]),
                     jnp.tile(w_pool[l].reshape(POOL_WIDTH, POOL_GDIM), (1, len(POOL_WINDOWS))), 0.0).astype(BF16)
    wo = w_out[l]
    w_out_p = jnp.concatenate([_pad_heads(wo[:MLSTM_WIDTH].T, M_HEADS, M_DK).T, wo[MLSTM_WIDTH:]], axis=0).astype(BF16)
    logit_bound = BOUND_SLACK * (HEAD_DIM ** 0.5) * jnp.max(jnp.abs(g_q[l])) * jnp.max(jnp.abs(g_k[l]))
    return dict(
        logit_bound=jnp.broadcast_to(logit_bound.astype(F32), (1, LANE)),
        g_mix=g_mix[l].reshape(1, D_MODEL), w_in=w_packed, gate_bias=gate_bias,
        g_q=jnp.tile(g_q[l], A_HEADS).reshape(1, ATTN_WIDTH), g_k=jnp.tile(g_k[l], A_HEADS).reshape(1, ATTN_WIDTH),
        bd64=bd64, g_mhead=_pad_heads(g_mhead[l].reshape(1, MLSTM_WIDTH), M_HEADS, M_DK),
        w_pool=w_bd, pool_scale=pool_scale[l].reshape(1, POOL_WIDTH),
        w_out=w_out_p, g_mlp=g_mlp[l].reshape(1, D_MODEL), w_up=w_up[l].astype(BF16), w_down=w_down[l].astype(BF16))


def _rope_tables(pos):
    t = pos.shape[0]
    half = ROPE_DIMS // 2
    inv = ROPE_THETA ** (-jnp.arange(half, dtype=F32) * 2.0 / ROPE_DIMS)
    ang = pos.astype(F32)[:, None] * inv[None, :]
    cos, sin = jnp.cos(ang), jnp.sin(ang)
    rest = HEAD_DIM - ROPE_DIMS
    blk_c = jnp.concatenate([cos, cos, jnp.ones((t, rest), F32)], axis=1)
    blk_s = jnp.concatenate([-sin, sin, jnp.zeros((t, rest), F32)], axis=1)
    pad_c, pad_s = jnp.ones((t, LANE - HEAD_DIM), F32), jnp.zeros((t, LANE - HEAD_DIM), F32)
    return (jnp.concatenate([blk_c, blk_c], axis=1), jnp.concatenate([blk_s, blk_s], axis=1),
            jnp.concatenate([blk_c, pad_c], axis=1), jnp.concatenate([blk_s, pad_s], axis=1))


def _unpad_state(c_aug, m):
    return c_aug[:, :, :M_DK, :M_DK], c_aug[:, :, :M_DK, N_COL], m[:, :, 0, 0]


def _prompt_layer(x, tabs, lw):
    t = x.shape[0]
    pr = _prep_call(x, tabs, lw)
    chunk = min(256, t)
    add_b = lambda a: a[None]
    y_m, c_n, m_n = _mlstm_call(add_b(pr["mq"]), add_b(pr["mk"]), add_b(pr["mv"]), add_b(pr["mo"]),
                                add_b(pr["gates"]), add_b(pr["gates_t"]), lw["g_mhead"],
                                jnp.zeros((1, M_HEADS, LANE, LANE), F32), jnp.zeros((1, M_HEADS, 1, LANE), F32), chunk)
    y_p = _pool_call(add_b(pr["pu"]), jnp.zeros((1, _HALO, POOL_WIDTH), F32), lw["w_pool"], lw["pool_scale"], 0)
    y_a = _dsa_prompt_call(pr["iq"], pr["aq"], pr["gates"], lw["logit_bound"], pr["ikb"], pr["akb"], pr["avb"])
    x_new = _finish_call(x, y_m[0], y_a, y_p[0], lw)
    return (x_new, pr["ak32"].reshape(1, t, A_HEADS, HEAD_DIM), pr["av32"].reshape(1, t, A_HEADS, HEAD_DIM),
            pr["ik32"].reshape(1, t, IDX_DIM), _unpad_state(c_n, m_n), pr["pu"][None, t - POOL_BUF:])


S_CHUNK = 128


def _sample_layer(layer, x, tabs, lw, st_c, st_n, st_m, st_pool, page_table, ci, ck, cv):
    b, t, _ = x.shape
    x2 = x.reshape(b * t, D_MODEL)
    pr = _prep_call(x2, tabs, lw)
    per = lambda a: a.reshape(b, t, a.shape[-1])
    pad_t = lambda a, n: jnp.pad(a, ((0, 0), (0, n - t), (0, 0)))

    gates = per(pr["gates"])
    pad_row = jnp.where(jnp.arange(LANE) < M_HEADS, NEG, 0.0).astype(F32)
    gates_p = jnp.concatenate([gates, jnp.broadcast_to(pad_row, (b, S_CHUNK - t, LANE))], axis=1)
    gates_t = jnp.swapaxes(gates_p[:, :, :8], 1, 2)
    c0 = jnp.pad(st_c, ((0, 0), (0, 0), (0, LANE - M_DK), (0, LANE - M_DK))).at[:, :, :M_DK, N_COL].set(st_n)
    m0 = jnp.broadcast_to(st_m[:, :, None, None], (b, M_HEADS, 1, LANE))
    y_m, c_n, m_n = _mlstm_call(pad_t(per(pr["mq"]), S_CHUNK), pad_t(per(pr["mk"]), S_CHUNK),
                                pad_t(per(pr["mv"]), S_CHUNK), pad_t(per(pr["mo"]), S_CHUNK),
                                gates_p, gates_t, lw["g_mhead"], c0, m0, S_CHUNK)

    past_len = page_table.shape[1] * ci.shape[-1]
    hist = jnp.concatenate([jnp.zeros((b, _HALO - POOL_BUF, POOL_WIDTH), F32), st_pool], axis=1)
    y_p = _pool_call(pad_t(per(pr["pu"]), ROWS_T), hist, lw["w_pool"], lw["pool_scale"], past_len)

    heads_first = lambda a: jnp.pad(jnp.swapaxes(a, 1, 2), ((0, 0), (0, 0), (0, ROWS_T - t), (0, 0)))
    iq = heads_first(per(pr["iq"]).reshape(b, t, IDX_HEADS, LANE)[..., :IDX_DIM])
    iq = iq.reshape(b, IDX_HEADS * ROWS_T, IDX_DIM)
    iw = heads_first(gates[:, :, G_IW:G_IW + IDX_HEADS, None]).reshape(b, IDX_HEADS * ROWS_T, 1)
    iw = jnp.broadcast_to(iw, (b, IDX_HEADS * ROWS_T, LANE))
    aq = heads_first(per(pr["aq"]).reshape(b, t, A_HEADS, HEAD_DIM))
    qbd = aq[:, :, :, None, :] * jnp.eye(A_HEADS, dtype=BF16)[None, :, None, :, None]
    qbd = qbd.reshape(b, A_HEADS * ROWS_T, ATTN_WIDTH)
    to_page = lambda a: jnp.pad(jnp.swapaxes(per(a), 1, 2), ((0, 0), (0, 0), (0, LANE - t)))
    y_a = _dsa_sample_call(layer, page_table, iq, iw, qbd, to_page(pr["ik32"]), to_page(pr["ak32"]),
                           to_page(pr["av32"]), ci, ck, cv, t)

    x_new = _finish_call(x2, y_m[:, :t].reshape(b * t, MP), y_a[:, :t].reshape(b * t, ATTN_WIDTH),
                         y_p[:, :t].reshape(b * t, POOL_WIDTH), lw)
    pool_buf = jnp.concatenate([st_pool, per(pr["pu"])], axis=1)[:, -POOL_BUF:]
    return (x_new.reshape(b, t, D_MODEL), pr["ak32"].reshape(b, t, A_HEADS, HEAD_DIM),
            pr["av32"].reshape(b, t, A_HEADS, HEAD_DIM), per(pr["ik32"]), _unpad_state(c_n, m_n), pool_buf)


def kernel(x_prompt, x_sample, cache_k, cache_v, cache_idx_k, state_C, state_n, state_m, state_pool, page_table,
           g_mix, w_in, b_i, b_f, g_q, g_k, g_mhead, w_pool, pool_scale, w_out, g_mlp, w_up, w_down):
    depth = w_in.shape[0]
    bp, s, _ = x_prompt.shape
    db, t, _ = x_sample.shape
    assert bp == 1 and t <= ROWS_T
    n_pool, page = cache_k.shape[1], cache_k.shape[2]
    past_len = page_table.shape[1] * page
    tabs_p = _rope_tables(jnp.arange(s, dtype=jnp.int32))
    tabs_s = _rope_tables(jnp.tile(past_len + jnp.arange(t, dtype=jnp.int32), db))
    ck = jnp.transpose(cache_k, (0, 1, 3, 4, 2)).reshape(depth, n_pool, ATTN_WIDTH, page)
    cv = jnp.transpose(cache_v, (0, 1, 3, 4, 2)).reshape(depth, n_pool, ATTN_WIDTH, page)
    ci = jnp.transpose(cache_idx_k, (0, 1, 3, 2))

    xp, xs = x_prompt[0], x_sample
    outs_p, outs_s = [], []
    for l in range(depth):
        lw = _pack_layer(l, g_mix, w_in, b_i, b_f, g_q, g_k, g_mhead, w_pool, pool_scale, w_out, g_mlp, w_up, w_down)
        xp, *rest_p = _prompt_layer(xp, tabs_p, lw)
        outs_p.append(rest_p)
        xs, *rest_s = _sample_layer(l, xs, tabs_s, lw, state_C[l], state_n[l], state_m[l], state_pool[l],
                                    page_table, ci, ck, cv)
        outs_s.append(rest_s)

    def stack(outs):
        k, v, ik, st, pool = zip(*outs)
        c, n, m = zip(*st)
        return [jnp.stack(a, axis=0) for a in (k, v, ik, c, n, m, pool)]

    return (xp[None], xs, *stack(outs_p), *stack(outs_s))
```

```python
import functools

import numpy as np
import jax
import jax.numpy as jnp
from jax import lax
from jax.experimental import pallas as pl
from jax.experimental.pallas import tpu as pltpu

F32 = jnp.float32
BF16 = jnp.bfloat16

D_MODEL = 1024
M_HEADS = 4
M_DK = 96
A_HEADS = 6
HEAD_DIM = 64
ATTN_WIDTH = A_HEADS * HEAD_DIM
IDX_HEADS = 4
IDX_DIM = 64
TOPK_MAX = 256
ROPE_THETA = 500000.0
ROPE_DIMS = HEAD_DIM // 4
POOL_WINDOWS = (2, 4, 8, 16)
POOL_GDIM = 64
POOL_WIDTH = 256
POOL_BUF = 15
D_FF = 4 * D_MODEL
EPS = 1e-6
NEG = -1e30
MLSTM_WIDTH = M_HEADS * M_DK
SPLIT_SIZES = (MLSTM_WIDTH, MLSTM_WIDTH, MLSTM_WIDTH, MLSTM_WIDTH, M_HEADS, M_HEADS,
               ATTN_WIDTH, ATTN_WIDTH, ATTN_WIDTH, IDX_HEADS * IDX_DIM, IDX_DIM, IDX_HEADS, POOL_WIDTH)
SPLIT_POINTS = tuple(int(s) for s in np.cumsum(SPLIT_SIZES)[:-1])

LANE = 128
MP = M_HEADS * LANE
N_COL = M_DK
OFF_MQ, OFF_MK, OFF_MV, OFF_MO = 0, MP, 2 * MP, 3 * MP
OFF_AQ = 4 * MP
OFF_AK = OFF_AQ + ATTN_WIDTH
OFF_AV = OFF_AK + ATTN_WIDTH
OFF_IQ = OFF_AV + ATTN_WIDTH
OFF_IK = OFF_IQ + IDX_HEADS * LANE
OFF_PU = OFF_IK + LANE
OFF_G = OFF_PU + POOL_WIDTH
W_PACKED = OFF_G + LANE
G_IG, G_LF, G_IW = 0, M_HEADS, 2 * M_HEADS

VMEM_LIMIT = 56 * 1024 * 1024
MAX_SEARCH = 21
MASK_VALUE = -1e30
MAX_SHIFT_BOUND = 40.0
BOUND_SLACK = 1.02

_NT = (((1,), (1,)), ((), ()))


def _cparams(sem):
    return pltpu.CompilerParams(dimension_semantics=sem, vmem_limit_bytes=VMEM_LIMIT)


def _const_spec(shape):
    nd = len(shape)
    return pl.BlockSpec(shape, lambda *_: (0,) * nd, pipeline_mode=pl.Buffered(1))


def _split3(a):
    a1 = a.astype(BF16)
    r1 = a - a1.astype(F32)
    a2 = r1.astype(BF16)
    a3 = (r1 - a2.astype(F32)).astype(BF16)
    return a1, a2, a3


def _dot_exact_lhs(a, b01):
    return sum(jnp.dot(t, b01, preferred_element_type=F32) for t in _split3(a))


def _dot_exact_rhs(a01, b):
    return sum(jnp.dot(a01, t, preferred_element_type=F32) for t in _split3(b))


def _rope_group(x, c, s, period):
    lane = lax.broadcasted_iota(jnp.int32, x.shape, 1)
    first_half = (lane % period) < (ROPE_DIMS // 2)
    partner = jnp.where(first_half, pltpu.roll(x, LANE - ROPE_DIMS // 2, 1), pltpu.roll(x, ROPE_DIMS // 2, 1))
    return x * c + partner * s


def _rope(x, c, s, period):
    groups = [_rope_group(x[:, g * LANE:(g + 1) * LANE], c, s, period) for g in range(x.shape[1] // LANE)]
    return groups[0] if len(groups) == 1 else jnp.concatenate(groups, axis=1)


def _prep_kernel(x_ref, g_ref, w_ref, bias_ref, gq_ref, gk_ref, c64_ref, s64_ref, c128_ref, s128_ref, bd_ref,
                 mq_o, mk_o, mv_o, mo_o, aq_o, ak32_o, akb_o, av32_o, avb_o, iq_o, ik32_o, ikb_o, pu_o,
                 gates_o, gates_t_o):
    x = x_ref[...]
    h = (x * lax.rsqrt(jnp.mean(x * x, axis=-1, keepdims=True) + EPS) * g_ref[...]).astype(BF16)

    def seg(off, width):
        return jnp.dot(h, w_ref[:, off:off + width], preferred_element_type=F32)

    tm = x.shape[0]
    lane_mp = lax.broadcasted_iota(jnp.int32, (tm, MP), 1)
    mq_o[...] = seg(OFF_MQ, MP).astype(BF16)
    mk_o[...] = (seg(OFF_MK, MP) * (M_DK ** -0.5)).astype(BF16)
    mv_o[...] = jnp.where(lane_mp % LANE == N_COL, 1.0, seg(OFF_MV, MP)).astype(BF16)
    mo_o[...] = seg(OFF_MO, MP)

    c64, s64, c128, s128 = c64_ref[...], s64_ref[...], c128_ref[...], s128_ref[...]
    bd = bd_ref[...]

    def qk_norm(z, gain):
        ms = _dot_exact_lhs(z * z, bd) * (1.0 / HEAD_DIM)
        return _rope(z * lax.rsqrt(ms + EPS) * gain, c64, s64, HEAD_DIM)

    aq = qk_norm(seg(OFF_AQ, ATTN_WIDTH), gq_ref[...])
    aq_o[...] = (aq * (HEAD_DIM ** -0.5)).astype(BF16)
    ak = qk_norm(seg(OFF_AK, ATTN_WIDTH), gk_ref[...])
    ak32_o[...] = ak
    akb_o[...] = ak.astype(BF16)
    av = seg(OFF_AV, ATTN_WIDTH)
    av32_o[...] = av
    avb_o[...] = av.astype(BF16)
    iq_o[...] = _rope(seg(OFF_IQ, IDX_HEADS * LANE), c128, s128, LANE).astype(BF16)
    ik = _rope(seg(OFF_IK, LANE), c128, s128, LANE)
    ik32_o[...] = ik[:, :IDX_DIM]
    ikb_o[...] = ik.astype(BF16)
    pu_o[...] = seg(OFF_PU, POOL_WIDTH)

    zg = seg(OFF_G, LANE)
    zb = zg + bias_ref[...]
    lane = lax.broadcasted_iota(jnp.int32, zg.shape, 1)
    log_sig = jnp.minimum(zb, 0.0) - jnp.log1p(jnp.exp(-jnp.abs(zb)))
    iw_scale = (IDX_HEADS ** -0.5) * (IDX_DIM ** -0.5)
    gates = jnp.where(lane < G_LF, zb,
                      jnp.where(lane < G_IW, log_sig,
                                jnp.where(lane < G_IW + IDX_HEADS, zg * iw_scale, 0.0)))
    gates_o[...] = gates
    gates_t_o[...] = gates.T[:8, :]


def _prep_call(x, pos_tabs, lw):
    t = x.shape[0]
    tm = min(256, t)
    row = lambda w: pl.BlockSpec((tm, w), lambda i: (i, 0))
    out_shapes = dict(
        mq=(MP, BF16), mk=(MP, BF16), mv=(MP, BF16), mo=(MP, F32),
        aq=(ATTN_WIDTH, BF16), ak32=(ATTN_WIDTH, F32), akb=(ATTN_WIDTH, BF16),
        av32=(ATTN_WIDTH, F32), avb=(ATTN_WIDTH, BF16),
        iq=(IDX_HEADS * LANE, BF16), ik32=(IDX_DIM, F32), ikb=(LANE, BF16),
        pu=(POOL_WIDTH, F32), gates=(LANE, F32))
    names = list(out_shapes)
    out_shape = [jax.ShapeDtypeStruct((t, w), dt) for w, dt in out_shapes.values()]
    out_specs = [row(w) for w, _ in out_shapes.values()]
    out_shape.append(jax.ShapeDtypeStruct((8, t), F32))
    out_specs.append(pl.BlockSpec((8, tm), lambda i: (0, i)))
    outs = pl.pallas_call(
        _prep_kernel,
        grid=(t // tm,),
        in_specs=[row(D_MODEL), _const_spec((1, D_MODEL)), _const_spec((D_MODEL, W_PACKED)),
                  _const_spec((1, LANE)), _const_spec((1, ATTN_WIDTH)), _const_spec((1, ATTN_WIDTH)),
                  row(LANE), row(LANE), row(LANE), row(LANE), _const_spec((ATTN_WIDTH, ATTN_WIDTH))],
        out_specs=out_specs,
        out_shape=out_shape,
        compiler_params=_cparams(("arbitrary",)),
        name="prep",
    )(x, lw["g_mix"], lw["w_in"], lw["gate_bias"], lw["g_q"], lw["g_k"], *pos_tabs, lw["bd64"])
    res = dict(zip(names, outs[:-1]))
    res["gates_t"] = outs[-1]
    return res


def _mlstm_kernel(q_ref, k_ref, v_ref, mo_ref, g_ref, gt_ref, gm_ref, c0_ref, m0_ref,
                  y_ref, cn_ref, mn_ref, c_scr, m_scr):
    j = pl.program_id(1)

    @pl.when(j == 0)
    def _():
        c_scr[...] = c0_ref[0]
        m_scr[...] = m0_ref[0]

    l = q_ref.shape[1]
    g = g_ref[0]
    gt = gt_ref[0]
    r_i = lax.broadcasted_iota(jnp.int32, (l, l), 0)
    c_i = lax.broadcasted_iota(jnp.int32, (l, l), 1)
    causal = c_i <= r_i
    tri_l = jnp.where(causal, 1.0, 0.0).astype(BF16)
    tri_u = jnp.where(r_i <= c_i, 1.0, 0.0).astype(BF16)
    bcum_c = _dot_exact_rhs(tri_l, g)
    bcum_r = _dot_exact_lhs(gt, tri_u)
    lane = lax.broadcasted_iota(jnp.int32, (l, LANE), 1)

    for h in range(M_HEADS):
        sl = slice(h * LANE, (h + 1) * LANE)
        qh, kh, vh = q_ref[0, :, sl], k_ref[0, :, sl], v_ref[0, :, sl]
        bc = bcum_c[:, G_LF + h:G_LF + h + 1]
        br = bcum_r[G_LF + h:G_LF + h + 1, :]
        ig_r = gt[G_IG + h:G_IG + h + 1, :]
        ig_c = g[:, G_IG + h:G_IG + h + 1]
        m_prev = m_scr[h][:, :1]
        c_prev = c_scr[h]

        log_d = jnp.where(causal, bc - br + ig_r, -jnp.inf)
        inter = bc + m_prev
        m_t = jnp.maximum(inter, jnp.max(log_d, axis=1, keepdims=True))
        s = lax.dot_general(qh, kh, _NT, preferred_element_type=F32) * jnp.exp(log_d - m_t)
        inter_w = jnp.exp(inter - m_t)
        num = (jnp.dot(s.astype(BF16), vh, preferred_element_type=F32)
               + inter_w * jnp.dot(qh, c_prev.astype(BF16), preferred_element_type=F32))
        den = num[:, N_COL:N_COL + 1]
        hval = num / jnp.maximum(jnp.abs(den), jnp.exp(-m_t))

        b_last = bc[l - 1:l, :]
        log_w = b_last - bc + ig_c
        m_new = jnp.maximum(b_last + m_prev, jnp.max(log_w, axis=0, keepdims=True))
        w_exp = jnp.exp(log_w - m_new)
        decay = jnp.exp(b_last + m_prev - m_new)
        kw_t = (kh.astype(F32) * w_exp).T.astype(BF16)
        c_scr[h] = decay * c_prev + jnp.dot(kw_t, vh, preferred_element_type=F32)
        m_scr[h] = jnp.broadcast_to(m_new, (1, LANE))

        o = jnp.where(lane < M_DK, jax.nn.sigmoid(mo_ref[0, :, sl]) * hval, 0.0)
        ms = jnp.sum(o * o, axis=1, keepdims=True) * (1.0 / M_DK)
        y_ref[0, :, sl] = (o * lax.rsqrt(ms + EPS) * gm_ref[:, sl]).astype(BF16)

    @pl.when(j == pl.num_programs(1) - 1)
    def _():
        cn_ref[0] = c_scr[...]
        mn_ref[0] = m_scr[...]


def _mlstm_call(q, k, v, mo, gates, gates_t, gm, c0, m0, chunk):
    b, t, _ = q.shape
    tok = lambda w: pl.BlockSpec((1, chunk, w), lambda i, j: (i, j, 0))
    st_c = pl.BlockSpec((1, M_HEADS, LANE, LANE), lambda i, j: (i, 0, 0, 0))
    st_m = pl.BlockSpec((1, M_HEADS, 1, LANE), lambda i, j: (i, 0, 0, 0))
    return pl.pallas_call(
        _mlstm_kernel,
        grid=(b, t // chunk),
        in_specs=[tok(MP), tok(MP), tok(MP), tok(MP), tok(LANE),
                  pl.BlockSpec((1, 8, chunk), lambda i, j: (i, 0, j)),
                  pl.BlockSpec((1, MP), lambda i, j: (0, 0)), st_c, st_m],
        out_specs=[tok(MP), st_c, st_m],
        out_shape=[jax.ShapeDtypeStruct((b, t, MP), BF16),
                   jax.ShapeDtypeStruct((b, M_HEADS, LANE, LANE), F32),
                   jax.ShapeDtypeStruct((b, M_HEADS, 1, LANE), F32)],
        scratch_shapes=[pltpu.VMEM((M_HEADS, LANE, LANE), F32), pltpu.VMEM((M_HEADS, 1, LANE), F32)],
        compiler_params=_cparams(("arbitrary", "arbitrary")),
        name="mlstm",
    )(q, k, v, mo, gates, gates_t, gm, c0, m0)


_HALO = 16


def _pool_kernel(u_ref, hist_ref, w_ref, sc_ref, y_ref, ext, *, pos0, carry):
    j = pl.program_id(1)
    tm = u_ref.shape[1]

    @pl.when(j == 0)
    def _():
        ext[0:_HALO, :] = hist_ref[0]

    ext[_HALO:_HALO + tm, :] = u_ref[0]
    cur = ext[_HALO:_HALO + tm, :]
    pos = pos0 + j * tm + lax.broadcasted_iota(jnp.int32, (tm, 1), 0)
    lane = lax.broadcasted_iota(jnp.int32, (tm, POOL_WIDTH), 1)
    run = cur
    d = None
    back = 1
    for gi, w in enumerate(POOL_WINDOWS):
        while back < w:
            run = run + ext[_HALO - back:_HALO - back + tm, :]
            back += 1
        cnt = jnp.minimum(w, pos + 1).astype(F32)
        dg = run / cnt - cur
        d = dg if d is None else jnp.where(lane >= gi * POOL_GDIM, dg, d)
    y = jnp.dot(d.astype(BF16), w_ref[...], preferred_element_type=F32) * sc_ref[...]
    y_ref[0] = y.astype(BF16)
    if carry:
        ext[0:_HALO, :] = ext[tm:tm + _HALO, :]


def _pool_call(u, hist, w_bd, scale, pos0):
    b, t, _ = u.shape
    tm = min(512, t)
    nt = t // tm
    return pl.pallas_call(
        functools.partial(_pool_kernel, pos0=pos0, carry=nt > 1),
        grid=(b, nt),
        in_specs=[pl.BlockSpec((1, tm, POOL_WIDTH), lambda i, j: (i, j, 0)),
                  pl.BlockSpec((1, _HALO, POOL_WIDTH), lambda i, j: (i, 0, 0)),
                  pl.BlockSpec((POOL_WIDTH, POOL_WIDTH), lambda i, j: (0, 0)),
                  pl.BlockSpec((1, POOL_WIDTH), lambda i, j: (0, 0))],
        out_specs=pl.BlockSpec((1, tm, POOL_WIDTH), lambda i, j: (i, j, 0)),
        out_shape=jax.ShapeDtypeStruct((b, t, POOL_WIDTH), BF16),
        scratch_shapes=[pltpu.VMEM((tm + _HALO, POOL_WIDTH), F32)],
        compiler_params=_cparams(("arbitrary", "arbitrary")),
        name="pool",
    )(u, hist, w_bd, scale)


_FF_CHUNK = 1024


def _finish_kernel(x_ref, mm_ref, ma_ref, mp_ref, wo_ref, g_ref, wu_ref, wd_ref, y_ref):
    o_a, o_p = MP, MP + ATTN_WIDTH
    xr = (x_ref[...]
          + jnp.dot(mm_ref[...], wo_ref[0:o_a, :], preferred_element_type=F32)
          + jnp.dot(ma_ref[...], wo_ref[o_a:o_p, :], preferred_element_type=F32)
          + jnp.dot(mp_ref[...], wo_ref[o_p:o_p + POOL_WIDTH, :], preferred_element_type=F32))
    hn = (xr * lax.rsqrt(jnp.mean(xr * xr, axis=-1, keepdims=True) + EPS) * g_ref[...]).astype(BF16)
    mlp = None
    for c in range(D_FF // _FF_CHUNK):
        sl = slice(c * _FF_CHUNK, (c + 1) * _FF_CHUNK)
        hid = jnp.maximum(jnp.dot(hn, wu_ref[:, sl], preferred_element_type=F32), 0.0)
        d = jnp.dot((hid * hid).astype(BF16), wd_ref[sl, :], preferred_element_type=F32)
        mlp = d if mlp is None else mlp + d
    y_ref[...] = xr + mlp


def _finish_call(x, mix_m, mix_a, mix_p, lw):
    t = x.shape[0]
    tm = min(512, t)
    row = lambda w: pl.BlockSpec((tm, w), lambda i: (i, 0))
    return pl.pallas_call(
        _finish_kernel,
        grid=(t // tm,),
        in_specs=[row(D_MODEL), row(MP), row(ATTN_WIDTH), row(POOL_WIDTH),
                  _const_spec((MP + ATTN_WIDTH + POOL_WIDTH, D_MODEL)), _const_spec((1, D_MODEL)),
                  _const_spec((D_MODEL, D_FF)), _const_spec((D_FF, D_MODEL))],
        out_specs=row(D_MODEL),
        out_shape=jax.ShapeDtypeStruct((t, D_MODEL), F32),
        compiler_params=_cparams(("arbitrary",)),
        name="finish",
    )(x, mix_m, mix_a, mix_p, lw["w_out"], lw["g_mlp"], lw["w_up"], lw["w_down"])


_BIG_SLOTS = 1e9


def _row_max(x):
    return jnp.max(x, axis=1, keepdims=True)


def _select_threshold(count_ge, max_below, mn, mx, n_adm, n_inadm, topk):
    kf = float(topk)
    g0 = n_adm + jnp.where(mn <= NEG, n_inadm, 0.0)
    all_kept = g0 < kf
    top = jnp.maximum(mx, NEG)
    hi0 = top + (jnp.abs(top) + 1.0) * 2.0 ** -10

    def search_cond(c):
        unsettled = jnp.where(all_kept | (c[3] == kf), 0.0, 1.0)
        return (c[0] < MAX_SEARCH) & (jnp.max(unsettled) > 0.5)

    def search(c):
        it, lo, hi, clo, chi = c
        mid = lo + (hi - lo) * 0.5
        cm = count_ge(mid)
        ge = cm >= kf
        return (it + 1, jnp.where(ge, mid, lo), jnp.where(ge, hi, mid),
                jnp.where(ge, cm, clo), jnp.where(ge, chi, cm))

    _, lo, hi, clo, chi = lax.while_loop(search_cond, search, (jnp.int32(0), mn, hi0, g0, jnp.zeros_like(mn)))
    thr0 = jnp.where(all_kept, mn, lo)
    done0 = jnp.where(all_kept | (clo == kf), 1.0, 0.0)

    def cond(c):
        return (c[0] <= topk) & (jnp.min(c[5]) < 0.5)

    def peel(c):
        it, (hi, chi, thr, slots, done, tie) = c[0], c[1:]
        v = max_below(hi)
        cv = count_ge(v)
        fin = (cv >= kf) & (done < 0.5)
        thr = jnp.where(fin, v, thr)
        slots = jnp.where(fin, kf - chi, slots)
        tie = jnp.where(fin & (cv > kf), 1.0, tie)
        done = jnp.where(fin, 1.0, done)
        live = done < 0.5
        return (it + 1, jnp.where(live, v, hi), jnp.where(live, cv, chi), thr, slots, done, tie)

    init = (jnp.int32(0), hi, chi, thr0, jnp.full_like(mn, _BIG_SLOTS), done0, jnp.zeros_like(mn))
    _, _, _, thr, slots, _, tie = lax.while_loop(cond, peel, init)
    return thr, slots, jnp.max(tie) > 0.5


KEY_GROUP = 2
_SUB = 8


def _dsa_prompt_kernel(iq_ref, aq_ref, g_ref, bound_ref, ik_ref, ak_ref, av_ref, o_ref,
                       s_scr, t_scr, iw_scr, qm_scr, mb_scr, acc_scr, m_scr, l_scr, *, tq, n_total, topk):
    i = pl.program_id(0)
    tk = tq
    nch = tk // LANE
    nrb = tq // _SUB
    grp = KEY_GROUP
    nkb = (i + grp) // grp
    q_pos = i * tq + lax.broadcasted_iota(jnp.int32, (tq, 1), 0)
    gates = g_ref[...]
    for h in range(IDX_HEADS):
        iw_scr[h] = jnp.broadcast_to(gates[:, G_IW + h:G_IW + h + 1], (tq, LANE))

    def tile_scores(kt):
        kk = ik_ref[pl.ds(pl.multiple_of(kt * tk, tk), tk), :]
        s = None
        for h in range(IDX_HEADS):
            sc = lax.dot_general(iq_ref[:, h * LANE:(h + 1) * LANE], kk, _NT, preferred_element_type=F32)
            w = iw_scr[h]
            term = jnp.concatenate([jnp.maximum(sc[:, c * LANE:(c + 1) * LANE], 0.0) * w for c in range(nch)], axis=1)
            s = term if s is None else s + term
        return s

    def fold(mx, mn, hi_side, lo_side):
        for c in range(nch):
            mx = jnp.maximum(mx, hi_side[:, c * LANE:(c + 1) * LANE])
            mn = jnp.minimum(mn, lo_side[:, c * LANE:(c + 1) * LANE])
        return mx, mn

    def score_block(kb, carry):
        mx, mn = carry
        for j in range(grp):
            kt = kb * grp + j
            s = tile_scores(kt)
            adm = (kt * tk + lax.broadcasted_iota(jnp.int32, (1, tk), 1)) <= q_pos
            s_masked = jnp.where(adm, s, -jnp.inf)
            s_scr[kt] = s_masked
            mx, mn = fold(mx, mn, s_masked, jnp.where(adm, s, jnp.inf))
        return mx, mn

    mx, mn = lax.fori_loop(0, nkb, score_block,
                           (jnp.full((tq, LANE), -jnp.inf, F32), jnp.full((tq, LANE), jnp.inf, F32)))
    mx = _row_max(mx)
    mn = jnp.min(mn, axis=1, keepdims=True)
    n_adm = (q_pos + 1).astype(F32)
    n_inadm = float(n_total) - n_adm

    def scan_scores(t, init, step):
        t_scr[...] = jnp.broadcast_to(t, (tq, LANE))

        def body(kb, accs):
            out = []
            for rb in range(nrb):
                rows = slice(rb * _SUB, (rb + 1) * _SUB)
                tb = t_scr[rows, :]
                a = accs[rb]
                for j in range(grp):
                    for c in range(nch):
                        a = step(a, s_scr[kb * grp + j, rows, c * LANE:(c + 1) * LANE], tb)
                out.append(a)
            return tuple(out)

        accs = lax.fori_loop(0, nkb, body, tuple(jnp.full((_SUB, LANE), init, F32) for _ in range(nrb)))
        return jnp.concatenate(accs, axis=0)

    def count_ge(t):
        acc = scan_scores(t, 0.0, lambda a, sv, tb: a + jnp.where(sv >= tb, 1.0, 0.0))
        return jnp.sum(acc, axis=1, keepdims=True) + jnp.where(t <= NEG, n_inadm, 0.0)

    def max_below(hi):
        acc = scan_scores(hi, -jnp.inf, lambda a, sv, tb: jnp.maximum(a, jnp.where(sv < tb, sv, -jnp.inf)))
        tail = jnp.where((hi > NEG) & (n_inadm > 0.0), NEG, -jnp.inf)
        return jnp.maximum(_row_max(acc), tail)

    thr, slots, any_tie = _select_threshold(count_ge, max_below, mn, mx, n_adm, n_inadm, topk)

    lane = lax.broadcasted_iota(jnp.int32, (tq, LANE), 1)
    for h in range(A_HEADS):
        qp = aq_ref[:, (h // 2) * LANE:(h // 2 + 1) * LANE].astype(F32)
        keep = (lane < HEAD_DIM) if h % 2 == 0 else (lane >= HEAD_DIM)
        qm_scr[h] = jnp.where(keep, qp, 0.0).astype(BF16)
    acc_scr[...] = jnp.zeros_like(acc_scr)
    l_scr[...] = jnp.zeros_like(l_scr)
    m_scr[...] = jnp.full_like(m_scr, MASK_VALUE)
    thr_b = jnp.broadcast_to(thr, (tq, LANE))

    tkb = grp * tk

    def attend(kb, kept):
        rows = pl.ds(pl.multiple_of(kb * tkb, tkb), tkb)
        for p in range(A_HEADS // 2):
            kp = ak_ref[rows, p * LANE:(p + 1) * LANE]
            vp = av_ref[rows, p * LANE:(p + 1) * LANE]
            for h in (2 * p, 2 * p + 1):
                lg = lax.dot_general(qm_scr[h], kp, _NT, preferred_element_type=F32)
                lgm = [jnp.where(kept[c], lg[:, c * LANE:(c + 1) * LANE], MASK_VALUE) for c in range(grp * nch)]
                cm = lgm[0]
                for x in lgm[1:]:
                    cm = jnp.maximum(cm, x)
                m_old = m_scr[h]
                m_new = jnp.maximum(m_old, _row_max(cm))
                alpha = jnp.exp(m_old - m_new)
                pc = [jnp.exp(x - m_new) for x in lgm]
                l_scr[h] = alpha * l_scr[h] + sum(pc)
                pmat = jnp.concatenate(pc, axis=1).astype(BF16)
                acc_scr[h] = alpha * acc_scr[h] + jnp.dot(pmat, vp, preferred_element_type=F32)
                m_scr[h] = m_new

    bound = bound_ref[...]
    neg_b = jnp.broadcast_to(-bound, (tq, LANE))

    r_i = lax.broadcasted_iota(jnp.int32, (tk, tk), 0)
    c_i = lax.broadcasted_iota(jnp.int32, (tk, tk), 1)
    slots_l = jnp.broadcast_to(slots, (tq, LANE))

    def bounded_path(tie_aware):
        def body(kb, seen):
            for j in range(grp):
                sv = s_scr[kb * grp + j]
                svc = [sv[:, c * LANE:(c + 1) * LANE] for c in range(nch)]
                if tie_aware:
                    before = jnp.where(r_i < c_i, 1.0, 0.0).astype(BF16)
                    eqc = [jnp.where(x == thr_b, 1.0, 0.0) for x in svc]
                    prior = jnp.dot(jnp.concatenate(eqc, axis=1).astype(BF16), before, preferred_element_type=F32)
                    mbc = [jnp.where(x > thr_b, neg_b,
                                     jnp.where(prior[:, c * LANE:(c + 1) * LANE] + seen < slots_l,
                                               jnp.where(x == thr_b, neg_b, -jnp.inf), -jnp.inf))
                           for c, x in enumerate(svc)]
                    seen = seen + jnp.sum(sum(eqc), axis=1, keepdims=True)
                else:
                    mbc = [jnp.where(x >= thr_b, neg_b, -jnp.inf) for x in svc]
                for c in range(nch):
                    cc = j * nch + c
                    mb_scr[:, cc * LANE:(cc + 1) * LANE] = mbc[c]
            rows = pl.ds(pl.multiple_of(kb * tkb, tkb), tkb)
            for p in range(A_HEADS // 2):
                kp = ak_ref[rows, p * LANE:(p + 1) * LANE]
                vp = av_ref[rows, p * LANE:(p + 1) * LANE]
                for h in (2 * p, 2 * p + 1):
                    lg = lax.dot_general(qm_scr[h], kp, _NT, preferred_element_type=F32)
                    pc = [jnp.exp(lg[:, c * LANE:(c + 1) * LANE] + mb_scr[:, c * LANE:(c + 1) * LANE])
                          for c in range(grp * nch)]
                    l_scr[h] = l_scr[h] + sum(pc)
                    pmat = jnp.concatenate(pc, axis=1).astype(BF16)
                    acc_scr[h] = acc_scr[h] + jnp.dot(pmat, vp, preferred_element_type=F32)
            return seen
        lax.fori_loop(0, nkb, body, jnp.zeros((tq, LANE), F32))

    def general_path():
        before = jnp.where(r_i < c_i, 1.0, 0.0).astype(BF16)
        slots_b = jnp.broadcast_to(slots, (tq, tk))
        thr_t = jnp.broadcast_to(thr, (tq, tk))

        def body(kb, seen):
            kept = []
            for j in range(grp):
                sv = s_scr[kb * grp + j]
                eq = jnp.where(sv == thr_t, 1.0, 0.0)
                prior = jnp.dot(eq.astype(BF16), before, preferred_element_type=F32) + seen
                kept_f = jnp.where(sv > thr_t, 1.0, jnp.where(prior < slots_b, eq, 0.0))
                kept += [kept_f[:, c * LANE:(c + 1) * LANE] > 0.5 for c in range(nch)]
                seen = seen + jnp.sum(eq, axis=1, keepdims=True)
            attend(kb, kept)
            return seen
        lax.fori_loop(0, nkb, body, jnp.zeros((tq, 1), F32))

    lax.cond(jnp.max(bound) <= MAX_SHIFT_BOUND,
             lambda: lax.cond(any_tie, lambda: bounded_path(True), lambda: bounded_path(False)),
             general_path)

    for p in range(A_HEADS // 2):
        o0 = acc_scr[2 * p] / jnp.sum(l_scr[2 * p], axis=1, keepdims=True)
        o1 = acc_scr[2 * p + 1] / jnp.sum(l_scr[2 * p + 1], axis=1, keepdims=True)
        o_ref[:, p * LANE:(p + 1) * LANE] = jnp.where(lane < HEAD_DIM, o0, o1).astype(BF16)


def _dsa_prompt_call(iq, aq, gates, bound, ikb, akb, avb):
    t = iq.shape[0]
    tq = min(256, t)
    assert (t // tq) % KEY_GROUP == 0
    topk = min(TOPK_MAX, t // 4)
    row = lambda w: pl.BlockSpec((tq, w), lambda i: (i, 0))
    return pl.pallas_call(
        functools.partial(_dsa_prompt_kernel, tq=tq, n_total=t, topk=topk),
        grid=(t // tq,),
        in_specs=[row(IDX_HEADS * LANE), row(ATTN_WIDTH), row(LANE), _const_spec((1, LANE)),
                  _const_spec((t, LANE)), _const_spec((t, ATTN_WIDTH)), _const_spec((t, ATTN_WIDTH))],
        out_specs=row(ATTN_WIDTH),
        out_shape=jax.ShapeDtypeStruct((t, ATTN_WIDTH), BF16),
        scratch_shapes=[pltpu.VMEM((t // tq, tq, tq), F32),
                        pltpu.VMEM((tq, LANE), F32),
                        pltpu.VMEM((IDX_HEADS, tq, LANE), F32),
                        pltpu.VMEM((A_HEADS, tq, LANE), BF16),
                        pltpu.VMEM((tq, KEY_GROUP * tq), F32),
                        pltpu.VMEM((A_HEADS, tq, LANE), F32),
                        pltpu.VMEM((A_HEADS, tq, LANE), F32),
                        pltpu.VMEM((A_HEADS, tq, LANE), F32)],
        compiler_params=_cparams(("arbitrary",)),
        name="dsa_prompt",
    )(iq, aq, gates, bound, ikb, akb, avb)


ROWS_T = 8
PAGES_PER_DMA = 16


def _dsa_sample_kernel(pt_ref, iq_ref, iw_ref, aq_ref, ikn_ref, kn_ref, vn_ref, ci_ref, ck_ref, cv_ref, o_ref,
                       s_scr, ibuf, kbuf, vbuf, isem, ksem, vsem, acc_scr, m_scr, l_scr,
                       *, layer, n_pages, n_new, past_len, topk):
    b = pl.program_id(0)
    page = LANE
    pg = PAGES_PER_DMA
    n_chunks = n_pages // pg
    a_rows = A_HEADS * ROWS_T

    def idx_copies(c, slot):
        return [pltpu.make_async_copy(ci_ref.at[layer, pt_ref[b, c * pg + p]], ibuf.at[slot, p], isem.at[slot])
                for p in range(pg)]

    def kv_copies(c, slot):
        cps = []
        for p in range(pg):
            phys = pt_ref[b, c * pg + p]
            cps.append(pltpu.make_async_copy(ck_ref.at[layer, phys], kbuf.at[slot, p], ksem.at[slot]))
            cps.append(pltpu.make_async_copy(cv_ref.at[layer, phys], vbuf.at[slot, p], vsem.at[slot]))
        return cps

    tok = lax.broadcasted_iota(jnp.int32, (ROWS_T, 1), 0)
    slot_i = lax.broadcasted_iota(jnp.int32, (1, page), 1)
    new_adm = (slot_i <= tok) & (slot_i < n_new)
    iq = iq_ref[0]
    iw = iw_ref[0]

    def page_scores(page_f32):
        sc = jnp.dot(iq, page_f32.astype(BF16), preferred_element_type=F32)
        r = jnp.maximum(sc, 0.0) * iw
        s = r[0:ROWS_T]
        for h in range(1, IDX_HEADS):
            s = s + r[h * ROWS_T:(h + 1) * ROWS_T]
        return s

    for cp in idx_copies(0, 0):
        cp.start()

    def score_chunk(c, carry):
        mx, mn = carry
        slot = c % 2

        @pl.when(c + 1 < n_chunks)
        def _():
            for cp in idx_copies(c + 1, 1 - slot):
                cp.start()

        for cp in idx_copies(c, slot):
            cp.wait()
        for p in range(pg):
            s = page_scores(ibuf[slot, p])
            s_scr[c * pg + p] = s
            mx = jnp.maximum(mx, s)
            mn = jnp.minimum(mn, s)
        return mx, mn

    mx, mn = lax.fori_loop(0, n_chunks, score_chunk,
                           (jnp.full((ROWS_T, page), -jnp.inf, F32), jnp.full((ROWS_T, page), jnp.inf, F32)))
    s_new = page_scores(ikn_ref[0])
    s_scr[n_pages] = jnp.where(new_adm, s_new, -jnp.inf)
    mx = _row_max(jnp.maximum(mx, jnp.where(new_adm, s_new, -jnp.inf)))
    mn = jnp.min(jnp.minimum(mn, jnp.where(new_adm, s_new, jnp.inf)), axis=1, keepdims=True)

    real = tok < n_new
    n_adm = jnp.where(real, (past_len + tok + 1).astype(F32), 0.0)
    n_inadm = jnp.where(real, (n_new - 1 - tok).astype(F32), 0.0)

    for cp in kv_copies(0, 0):
        cp.start()

    def scan_pages(init, step):
        lanes = [jnp.full((ROWS_T, page), init, F32) for _ in range(4)]
        for j in range(n_pages + 1):
            lanes[j % 4] = step(lanes[j % 4], s_scr[j])
        return lanes

    def count_ge(t):
        tb = jnp.broadcast_to(t, (ROWS_T, page))
        parts = scan_pages(0.0, lambda a, sv: a + jnp.where(sv >= tb, 1.0, 0.0))
        acc = (parts[0] + parts[1]) + (parts[2] + parts[3])
        return jnp.sum(acc, axis=1, keepdims=True) + jnp.where(t <= NEG, n_inadm, 0.0)

    def max_below(hi):
        hb = jnp.broadcast_to(hi, (ROWS_T, page))
        parts = scan_pages(-jnp.inf, lambda a, sv: jnp.maximum(a, jnp.where(sv < hb, sv, -jnp.inf)))
        acc = jnp.maximum(jnp.maximum(parts[0], parts[1]), jnp.maximum(parts[2], parts[3]))
        tail = jnp.where((hi > NEG) & (n_inadm > 0.0), NEG, -jnp.inf)
        return jnp.maximum(_row_max(acc), tail)

    thr, slots, any_tie = _select_threshold(count_ge, max_below, mn, mx, n_adm, n_inadm, topk)

    acc_scr[...] = jnp.zeros_like(acc_scr)
    l_scr[...] = jnp.zeros_like(l_scr)
    m_scr[...] = jnp.full_like(m_scr, MASK_VALUE)
    qbd = aq_ref[0]
    thr_b = jnp.broadcast_to(thr, (ROWS_T, page))
    slots_b = jnp.broadcast_to(slots, (ROWS_T, page))
    r_i = lax.broadcasted_iota(jnp.int32, (page, page), 0)
    c_i = lax.broadcasted_iota(jnp.int32, (page, page), 1)
    before = jnp.where(r_i < c_i, 1.0, 0.0).astype(BF16)

    def kept_masks(pages, seen):
        svs = [s_scr[j] for j in pages]

        def plain():
            return [jnp.where(sv >= thr_b, 1.0, 0.0) for sv in svs], seen

        def tie_aware():
            eqs = [jnp.where(sv == thr_b, 1.0, 0.0) for sv in svs]
            local = [jnp.dot(eq.astype(BF16), before, preferred_element_type=F32) for eq in eqs]
            counts = [jnp.sum(eq, axis=1, keepdims=True) for eq in eqs]
            run, out = seen, []
            for sv, eq, loc, cnt in zip(svs, eqs, local, counts):
                out.append(jnp.where(sv > thr_b, 1.0, jnp.where(loc + run < slots_b, eq, 0.0)))
                run = run + cnt
            return out, run

        return lax.cond(any_tie, tie_aware, plain)

    def attend(k_pages, v_pages, kept):
        lgm = []
        for kt, km in zip(k_pages, kept):
            lg = jnp.dot(qbd, kt.astype(BF16), preferred_element_type=F32)
            km_all = jnp.concatenate([km] * A_HEADS, axis=0)
            lgm.append(jnp.where(km_all > 0.5, lg, MASK_VALUE))
        cm = lgm[0]
        for x in lgm[1:]:
            cm = jnp.maximum(cm, x)
        m_old = m_scr[...]
        m_new = jnp.maximum(m_old, _row_max(cm))
        alpha = jnp.exp(m_old - m_new)
        pc = [jnp.exp(x - m_new) for x in lgm]
        l_scr[...] = alpha * l_scr[...] + sum(pc)
        pv = None
        for p_t, vt in zip(pc, v_pages):
            d = lax.dot_general(p_t.astype(BF16), vt.astype(BF16), _NT, preferred_element_type=F32)
            pv = d if pv is None else pv + d
        acc_scr[...] = alpha[:, :1] * acc_scr[...] + pv
        m_scr[...] = m_new

    def chunk(c, seen):
        slot = c % 2

        @pl.when(c + 1 < n_chunks)
        def _():
            for cp in kv_copies(c + 1, 1 - slot):
                cp.start()

        for cp in kv_copies(c, slot):
            cp.wait()
        kept, seen = kept_masks([c * pg + p for p in range(pg)], seen)
        attend([kbuf[slot, p] for p in range(pg)], [vbuf[slot, p] for p in range(pg)], kept)
        return seen

    seen = lax.fori_loop(0, n_chunks, chunk, jnp.zeros((ROWS_T, 1), F32))
    kept, _ = kept_masks([n_pages], seen)
    attend([kn_ref[0]], [vn_ref[0]], kept)

    out_rows = acc_scr[...] / jnp.sum(l_scr[...], axis=1, keepdims=True)
    lane = lax.broadcasted_iota(jnp.int32, (ROWS_T, ATTN_WIDTH), 1)
    out = jnp.zeros((ROWS_T, ATTN_WIDTH), F32)
    for h in range(A_HEADS):
        out = jnp.where(lane // HEAD_DIM == h, out_rows[h * ROWS_T:(h + 1) * ROWS_T], out)
    o_ref[0] = out.astype(BF16)


def _dsa_sample_call(layer, page_table, iq, iw, aq, ik_new, k_new, v_new, ci, ck, cv, n_new):
    b, n_pages = page_table.shape
    page = ci.shape[-1]
    assert page == LANE and n_pages % PAGES_PER_DMA == 0
    past_len = n_pages * page
    topk = min(TOPK_MAX, (past_len + n_new) // 4)
    a_rows = A_HEADS * ROWS_T
    bspec = lambda r, w: pl.BlockSpec((1, r, w), lambda i, pt: (i, 0, 0))
    hbm = pl.BlockSpec(memory_space=pl.ANY)
    pg = PAGES_PER_DMA
    return pl.pallas_call(
        functools.partial(_dsa_sample_kernel, layer=layer, n_pages=n_pages, n_new=n_new, past_len=past_len, topk=topk),
        grid_spec=pltpu.PrefetchScalarGridSpec(
            num_scalar_prefetch=1,
            grid=(b,),
            in_specs=[bspec(IDX_HEADS * ROWS_T, IDX_DIM), bspec(IDX_HEADS * ROWS_T, LANE), bspec(a_rows, ATTN_WIDTH),
                      bspec(IDX_DIM, page), bspec(ATTN_WIDTH, page), bspec(ATTN_WIDTH, page), hbm, hbm, hbm],
            out_specs=bspec(ROWS_T, ATTN_WIDTH),
            scratch_shapes=[pltpu.VMEM((n_pages + 1, ROWS_T, page), F32),
                            pltpu.VMEM((2, pg, IDX_DIM, page), F32),
                            pltpu.VMEM((2, pg, ATTN_WIDTH, page), F32),
                            pltpu.VMEM((2, pg, ATTN_WIDTH, page), F32),
                            pltpu.SemaphoreType.DMA((2,)), pltpu.SemaphoreType.DMA((2,)), pltpu.SemaphoreType.DMA((2,)),
                            pltpu.VMEM((a_rows, ATTN_WIDTH), F32),
                            pltpu.VMEM((a_rows, page), F32),
                            pltpu.VMEM((a_rows, page), F32)]),
        out_shape=jax.ShapeDtypeStruct((b, ROWS_T, ATTN_WIDTH), BF16),
        compiler_params=_cparams(("arbitrary",)),
        name="dsa_sample",
    )(page_table, iq, iw, aq, ik_new, k_new, v_new, ci, ck, cv)


def _pad_heads(a, n_heads, head_dim):
    r = a.shape[0]
    a = a.reshape(r, n_heads, head_dim)
    return jnp.pad(a, ((0, 0), (0, 0), (0, LANE - head_dim))).reshape(r, n_heads * LANE)


def _pack_layer(l, g_mix, w_in, b_i, b_f, g_q, g_k, g_mhead, w_pool, pool_scale, w_out, g_mlp, w_up, w_down):
    mq, mk, mv, mo, mi, mf, aq, ak, av, iq, ik, iw, pu = jnp.split(w_in[l], SPLIT_POINTS, axis=1)
    gate_w = jnp.pad(jnp.concatenate([mi, mf, iw], axis=1), ((0, 0), (0, LANE - 3 * M_HEADS)))
    w_packed = jnp.concatenate(
        [_pad_heads(mq, M_HEADS, M_DK), _pad_heads(mk, M_HEADS, M_DK), _pad_heads(mv, M_HEADS, M_DK),
         _pad_heads(mo, M_HEADS, M_DK), aq, ak, av, _pad_heads(iq, IDX_HEADS, IDX_DIM),
         jnp.pad(ik, ((0, 0), (0, LANE - IDX_DIM))), pu, gate_w], axis=1).astype(BF16)
    gate_bias = jnp.pad(jnp.concatenate([b_i[l], b_f[l]]), (0, LANE - 2 * M_HEADS)).reshape(1, LANE)
    head_id = np.arange(ATTN_WIDTH) // HEAD_DIM
    bd64 = jnp.asarray(head_id[:, None] == head_id[None, :], BF16)
    grp = np.arange(POOL_WIDTH) // POOL_GDIM
    w_bd = jnp.where(jnp.asarray(grp[:, None] == grp[None, :]),
                     jnp.tile(w_pool[l].reshape(POOL_WIDTH, POOL_GDIM), (1, len(POOL_WINDOWS))), 0.0).astype(BF16)
    wo = w_out[l]
    w_out_p = jnp.concatenate([_pad_heads(wo[:MLSTM_WIDTH].T, M_HEADS, M_DK).T, wo[MLSTM_WIDTH:]], axis=0).astype(BF16)
    logit_bound = BOUND_SLACK * (HEAD_DIM ** 0.5) * jnp.max(jnp.abs(g_q[l])) * jnp.max(jnp.abs(g_k[l]))
    return dict(
        logit_bound=jnp.broadcast_to(logit_bound.astype(F32), (1, LANE)),
        g_mix=g_mix[l].reshape(1, D_MODEL), w_in=w_packed, gate_bias=gate_bias,
        g_q=jnp.tile(g_q[l], A_HEADS).reshape(1, ATTN_WIDTH), g_k=jnp.tile(g_k[l], A_HEADS).reshape(1, ATTN_WIDTH),
        bd64=bd64, g_mhead=_pad_heads(g_mhead[l].reshape(1, MLSTM_WIDTH), M_HEADS, M_DK),
        w_pool=w_bd, pool_scale=pool_scale[l].reshape(1, POOL_WIDTH),
        w_out=w_out_p, g_mlp=g_mlp[l].reshape(1, D_MODEL), w_up=w_up[l].astype(BF16), w_down=w_down[l].astype(BF16))


def _rope_tables(pos):
    t = pos.shape[0]
    half = ROPE_DIMS // 2
    inv = ROPE_THETA ** (-jnp.arange(half, dtype=F32) * 2.0 / ROPE_DIMS)
    ang = pos.astype(F32)[:, None] * inv[None, :]
    cos, sin = jnp.cos(ang), jnp.sin(ang)
    rest = HEAD_DIM - ROPE_DIMS
    blk_c = jnp.concatenate([cos, cos, jnp.ones((t, rest), F32)], axis=1)
    blk_s = jnp.concatenate([-sin, sin, jnp.zeros((t, rest), F32)], axis=1)
    pad_c, pad_s = jnp.ones((t, LANE - HEAD_DIM), F32), jnp.zeros((t, LANE - HEAD_DIM), F32)
    return (jnp.concatenate([blk_c, blk_c], axis=1), jnp.concatenate([blk_s, blk_s], axis=1),
            jnp.concatenate([blk_c, pad_c], axis=1), jnp.concatenate([blk_s, pad_s], axis=1))


def _unpad_state(c_aug, m):
    return c_aug[:, :, :M_DK, :M_DK], c_aug[:, :, :M_DK, N_COL], m[:, :, 0, 0]


def _prompt_layer(x, tabs, lw):
    t = x.shape[0]
    pr = _prep_call(x, tabs, lw)
    chunk = min(256, t)
    add_b = lambda a: a[None]
    y_m, c_n, m_n = _mlstm_call(add_b(pr["mq"]), add_b(pr["mk"]), add_b(pr["mv"]), add_b(pr["mo"]),
                                add_b(pr["gates"]), add_b(pr["gates_t"]), lw["g_mhead"],
                                jnp.zeros((1, M_HEADS, LANE, LANE), F32), jnp.zeros((1, M_HEADS, 1, LANE), F32), chunk)
    y_p = _pool_call(add_b(pr["pu"]), jnp.zeros((1, _HALO, POOL_WIDTH), F32), lw["w_pool"], lw["pool_scale"], 0)
    y_a = _dsa_prompt_call(pr["iq"], pr["aq"], pr["gates"], lw["logit_bound"], pr["ikb"], pr["akb"], pr["avb"])
    x_new = _finish_call(x, y_m[0], y_a, y_p[0], lw)
    return (x_new, pr["ak32"].reshape(1, t, A_HEADS, HEAD_DIM), pr["av32"].reshape(1, t, A_HEADS, HEAD_DIM),
            pr["ik32"].reshape(1, t, IDX_DIM), _unpad_state(c_n, m_n), pr["pu"][None, t - POOL_BUF:])


S_CHUNK = 128


def _sample_layer(layer, x, tabs, lw, st_c, st_n, st_m, st_pool, page_table, ci, ck, cv):
    b, t, _ = x.shape
    x2 = x.reshape(b * t, D_MODEL)
    pr = _prep_call(x2, tabs, lw)
    per = lambda a: a.reshape(b, t, a.shape[-1])
    pad_t = lambda a, n: jnp.pad(a, ((0, 0), (0, n - t), (0, 0)))

    gates = per(pr["gates"])
    pad_row = jnp.where(jnp.arange(LANE) < M_HEADS, NEG, 0.0).astype(F32)
    gates_p = jnp.concatenate([gates, jnp.broadcast_to(pad_row, (b, S_CHUNK - t, LANE))], axis=1)
    gates_t = jnp.swapaxes(gates_p[:, :, :8], 1, 2)
    c0 = jnp.pad(st_c, ((0, 0), (0, 0), (0, LANE - M_DK), (0, LANE - M_DK))).at[:, :, :M_DK, N_COL].set(st_n)
    m0 = jnp.broadcast_to(st_m[:, :, None, None], (b, M_HEADS, 1, LANE))
    y_m, c_n, m_n = _mlstm_call(pad_t(per(pr["mq"]), S_CHUNK), pad_t(per(pr["mk"]), S_CHUNK),
                                pad_t(per(pr["mv"]), S_CHUNK), pad_t(per(pr["mo"]), S_CHUNK),
                                gates_p, gates_t, lw["g_mhead"], c0, m0, S_CHUNK)

    past_len = page_table.shape[1] * ci.shape[-1]
    hist = jnp.concatenate([jnp.zeros((b, _HALO - POOL_BUF, POOL_WIDTH), F32), st_pool], axis=1)
    y_p = _pool_call(pad_t(per(pr["pu"]), ROWS_T), hist, lw["w_pool"], lw["pool_scale"], past_len)

    heads_first = lambda a: jnp.pad(jnp.swapaxes(a, 1, 2), ((0, 0), (0, 0), (0, ROWS_T - t), (0, 0)))
    iq = heads_first(per(pr["iq"]).reshape(b, t, IDX_HEADS, LANE)[..., :IDX_DIM])
    iq = iq.reshape(b, IDX_HEADS * ROWS_T, IDX_DIM)
    iw = heads_first(gates[:, :, G_IW:G_IW + IDX_HEADS, None]).reshape(b, IDX_HEADS * ROWS_T, 1)
    iw = jnp.broadcast_to(iw, (b, IDX_HEADS * ROWS_T, LANE))
    aq = heads_first(per(pr["aq"]).reshape(b, t, A_HEADS, HEAD_DIM))
    qbd = aq[:, :, :, None, :] * jnp.eye(A_HEADS, dtype=BF16)[None, :, None, :, None]
    qbd = qbd.reshape(b, A_HEADS * ROWS_T, ATTN_WIDTH)
    to_page = lambda a: jnp.pad(jnp.swapaxes(per(a), 1, 2), ((0, 0), (0, 0), (0, LANE - t)))
    y_a = _dsa_sample_call(layer, page_table, iq, iw, qbd, to_page(pr["ik32"]), to_page(pr["ak32"]),
                           to_page(pr["av32"]), ci, ck, cv, t)

    x_new = _finish_call(x2, y_m[:, :t].reshape(b * t, MP), y_a[:, :t].reshape(b * t, ATTN_WIDTH),
                         y_p[:, :t].reshape(b * t, POOL_WIDTH), lw)
    pool_buf = jnp.concatenate([st_pool, per(pr["pu"])], axis=1)[:, -POOL_BUF:]
    return (x_new.reshape(b, t, D_MODEL), pr["ak32"].reshape(b, t, A_HEADS, HEAD_DIM),
            pr["av32"].reshape(b, t, A_HEADS, HEAD_DIM), per(pr["ik32"]), _unpad_state(c_n, m_n), pool_buf)


def kernel(x_prompt, x_sample, cache_k, cache_v, cache_idx_k, state_C, state_n, state_m, state_pool, page_table,
           g_mix, w_in, b_i, b_f, g_q, g_k, g_mhead, w_pool, pool_scale, w_out, g_mlp, w_up, w_down):
    depth = w_in.shape[0]
    bp, s, _ = x_prompt.shape
    db, t, _ = x_sample.shape
    assert bp == 1 and t <= ROWS_T
    n_pool, page = cache_k.shape[1], cache_k.shape[2]
    past_len = page_table.shape[1] * page
    tabs_p = _rope_tables(jnp.arange(s, dtype=jnp.int32))
    tabs_s = _rope_tables(jnp.tile(past_len + jnp.arange(t, dtype=jnp.int32), db))
    ck = jnp.transpose(cache_k, (0, 1, 3, 4, 2)).reshape(depth, n_pool, ATTN_WIDTH, page)
    cv = jnp.transpose(cache_v, (0, 1, 3, 4, 2)).reshape(depth, n_pool, ATTN_WIDTH, page)
    ci = jnp.transpose(cache_idx_k, (0, 1, 3, 2))

    xp, xs = x_prompt[0], x_sample
    outs_p, outs_s = [], []
    for l in range(depth):
        lw = _pack_layer(l, g_mix, w_in, b_i, b_f, g_q, g_k, g_mhead, w_pool, pool_scale, w_out, g_mlp, w_up, w_down)
        xp, *rest_p = _prompt_layer(xp, tabs_p, lw)
        outs_p.append(rest_p)
        xs, *rest_s = _sample_layer(l, xs, tabs_s, lw, state_C[l], state_n[l], state_m[l], state_pool[l],
                                    page_table, ci, ck, cv)
        outs_s.append(rest_s)

    def stack(outs):
        k, v, ik, st, pool = zip(*outs)
        c, n, m = zip(*st)
        return [jnp.stack(a, axis=0) for a in (k, v, ik, c, n, m, pool)]

    return (xp[None], xs, *stack(outs_p), *stack(outs_s))
```
